```python
import math
import jax
import jax.numpy as jnp
from jax import lax
import numpy as np

D_MODEL = 2048
BATCH = 4
SEQ = 4096
DEPTH = 1

EPS = 1e-6
NEG_INF = -1e30
MLA_HEADS = 8
MLA_NOPE = 128
MLA_ROPE = 64
MLA_V = 128
MLA_Q_LORA = 512
MLA_KV_LORA = 512
ROPE_THETA = 10000.0
ATTN_Q_BLOCK = 128
MOBA_HEADS = 8
MOBA_HEAD_DIM = 128
MOBA_BLOCK = 256
MOBA_TOPK = 3
MOBA_Q_CHUNK = 64
REL_BUCKETS = 32
REL_MAX_DIST = 128
N_EXPERTS = 32
TOP_K = 4
D_EXPERT = D_MODEL
SWIGLU_LIMIT = 7.0
SWIGLU_ALPHA = 1.702
EXPERT_ROW_BLOCK = 256
PLE_DIM = 256
IN_SIZES = (MLA_Q_LORA, MLA_KV_LORA, MLA_ROPE,
            MOBA_HEADS * MOBA_HEAD_DIM, MOBA_HEADS * MOBA_HEAD_DIM, MOBA_HEADS * MOBA_HEAD_DIM,
            D_MODEL, D_MODEL)
IN_COLS = MLA_Q_LORA + MLA_KV_LORA + MLA_ROPE + 3 * MOBA_HEADS * MOBA_HEAD_DIM + 2 * D_MODEL

kernel_name = 'hybrid_mla_moba_moe_ple_block'


def rms_norm(x, g):
    xf = x.astype(jnp.float32)
    y = xf * lax.rsqrt(jnp.mean(xf * xf, axis=-1, keepdims=True) + EPS)
    return (y * g.astype(jnp.float32)).astype(x.dtype)


def split_cols(z, sizes):
    parts, start = [], 0
    for s in sizes:
        parts.append(z[..., start:start + s])
        start += s
    return parts


def rope_tables(seq):
    inv = 1.0 / (ROPE_THETA ** (jnp.arange(0, MLA_ROPE, 2, dtype=jnp.float32) / MLA_ROPE))
    ang = jnp.arange(seq, dtype=jnp.float32)[:, None] * inv[None, :]
    return jnp.cos(ang), jnp.sin(ang)


def apply_rope(x, cos, sin):
    half = x.shape[-1] // 2
    x1, x2 = x[..., :half], x[..., half:]
    c, s = cos.astype(x.dtype), sin.astype(x.dtype)
    return jnp.concatenate([x1 * c - x2 * s, x1 * s + x2 * c], axis=-1)


def t5_bucket(dist):
    n = jnp.maximum(dist, 0)
    max_exact = REL_BUCKETS // 2
    large = max_exact + (jnp.log(jnp.maximum(n, 1).astype(jnp.float32) / max_exact)
                         / math.log(REL_MAX_DIST / max_exact)
                         * (REL_BUCKETS - max_exact)).astype(jnp.int32)
    large = jnp.minimum(large, REL_BUCKETS - 1)
    return jnp.where(n < max_exact, n, large)


def causal_block_attention(q, k, v, scale):
    B, S, H, Dk = q.shape
    nq = S // ATTN_Q_BLOCK
    qb = jnp.moveaxis(q.reshape(B, nq, ATTN_Q_BLOCK, H, Dk), 1, 0)
    kpos = jnp.arange(S)

    def one_block(args):
        qi, i = args
        s = jnp.einsum('bqhd,bkhd->bhqk', qi, k).astype(jnp.float32) * scale
        qpos = i * ATTN_Q_BLOCK + jnp.arange(ATTN_Q_BLOCK)
        s = jnp.where(kpos[None, :] <= qpos[:, None], s, NEG_INF)
        pr = jax.nn.softmax(s, axis=-1).astype(v.dtype)
        return jnp.einsum('bhqk,bkhd->bqhd', pr, v)

    out = lax.map(one_block, (qb, jnp.arange(nq)))
    return jnp.moveaxis(out, 0, 1).reshape(B, S, H, v.shape[-1])


def mla_attention(q_lat, kv_lat, k_rope, q_lat_norm, kv_lat_norm, w_uq, w_ukv):
    B, S, _ = q_lat.shape
    q = (rms_norm(q_lat, q_lat_norm) @ w_uq).reshape(B, S, MLA_HEADS, MLA_NOPE + MLA_ROPE)
    kv = (rms_norm(kv_lat, kv_lat_norm) @ w_ukv).reshape(B, S, MLA_HEADS, MLA_NOPE + MLA_V)
    cos, sin = rope_tables(S)
    q_rope = apply_rope(q[..., MLA_NOPE:], cos[:, None, :], sin[:, None, :])
    k_rope = apply_rope(k_rope, cos, sin)
    q = jnp.concatenate([q[..., :MLA_NOPE], q_rope], axis=-1)
    k = jnp.concatenate([kv[..., :MLA_NOPE],
                         jnp.broadcast_to(k_rope[:, :, None, :], (B, S, MLA_HEADS, MLA_ROPE))], axis=-1)
    v = kv[..., MLA_NOPE:]
    out = causal_block_attention(q, k, v, (MLA_NOPE + MLA_ROPE) ** -0.5)
    return out.reshape(B, S, MLA_HEADS * MLA_V)


def moba_attention(q, k, v, rel_bias):
    B, S, H, D = q.shape
    Sp = ((S + MOBA_BLOCK - 1) // MOBA_BLOCK) * MOBA_BLOCK
    pad = ((0, 0), (0, Sp - S), (0, 0), (0, 0))
    q, k, v = jnp.pad(q, pad), jnp.pad(k, pad), jnp.pad(v, pad)
    nb = Sp // MOBA_BLOCK
    nc = Sp // MOBA_Q_CHUNK
    n_sel = min(MOBA_TOPK, nb)
    qh = q.transpose(0, 2, 1, 3)
    kbh = k.transpose(0, 2, 1, 3).reshape(B, H, nb, MOBA_BLOCK, D)
    vbh = v.transpose(0, 2, 1, 3).reshape(B, H, nb, MOBA_BLOCK, D)
    k_mean = jnp.mean(kbh.astype(jnp.float32), axis=3)
    gate = jnp.einsum('bhsd,bhnd->bhsn', qh.astype(jnp.float32), k_mean)
    q_blk = jnp.arange(Sp) // MOBA_BLOCK
    past = jnp.arange(nb)[None, :] < q_blk[:, None]
    gate = jnp.where(past, gate, NEG_INF)
    _, sel = lax.top_k(gate, n_sel)
    sel_valid = sel < q_blk[:, None]

    def to_chunks(a):
        return jnp.moveaxis(a.reshape((B, H, nc, MOBA_Q_CHUNK) + a.shape[3:]), 2, 0)

    b_ix = jnp.arange(B)[:, None, None, None]
    h_ix = jnp.arange(H)[None, :, None, None]
    h_ix5 = jnp.arange(H)[None, :, None, None, None]
    offs = jnp.arange(MOBA_BLOCK)
    table_h = rel_bias.T
    scale = D ** -0.5

    def one_chunk(args):
        qc, selc, validc, c = args
        qpos = c * MOBA_Q_CHUNK + jnp.arange(MOBA_Q_CHUNK)
        gk = kbh[b_ix, h_ix, selc]
        gv = vbh[b_ix, h_ix, selc]
        s_sel = jnp.einsum('bhqd,bhqtkd->bhqtk', qc, gk).astype(jnp.float32) * scale
        kpos_sel = selc[..., None] * MOBA_BLOCK + offs
        s_sel = s_sel + table_h[h_ix5, t5_bucket(qpos[:, None, None] - kpos_sel)].astype(jnp.float32)
        s_sel = jnp.where(validc[..., None], s_sel, NEG_INF).reshape(B, H, MOBA_Q_CHUNK, n_sel * MOBA_BLOCK)
        j = (c * MOBA_Q_CHUNK) // MOBA_BLOCK
        ko = lax.dynamic_index_in_dim(kbh, j, axis=2, keepdims=False)
        vo = lax.dynamic_index_in_dim(vbh, j, axis=2, keepdims=False)
        dist_own = qpos[:, None] - (j * MOBA_BLOCK + offs)[None, :]
        s_own = jnp.einsum('bhqd,bhkd->bhqk', qc, ko).astype(jnp.float32) * scale
        s_own = s_own + table_h[:, t5_bucket(dist_own)].astype(jnp.float32)
        s_own = jnp.where(dist_own >= 0, s_own, NEG_INF)
        pr = jax.nn.softmax(jnp.concatenate([s_sel, s_own], axis=-1), axis=-1).astype(v.dtype)
        p_sel = pr[..., :n_sel * MOBA_BLOCK].reshape(B, H, MOBA_Q_CHUNK, n_sel, MOBA_BLOCK)
        p_own = pr[..., n_sel * MOBA_BLOCK:]
        return (jnp.einsum('bhqtk,bhqtkd->bhqd', p_sel, gv)
                + jnp.einsum('bhqk,bhkd->bhqd', p_own, vo))

    outs = lax.map(one_chunk, (to_chunks(qh), to_chunks(sel), to_chunks(sel_valid), jnp.arange(nc)))
    out = jnp.moveaxis(outs, 0, 2).reshape(B, H, Sp, D).transpose(0, 2, 1, 3)
    return out[:, :S]


def clamped_swiglu(g, u):
    g = jnp.minimum(g, SWIGLU_LIMIT)
    u = jnp.clip(u, -SWIGLU_LIMIT, SWIGLU_LIMIT)
    return g * jax.nn.sigmoid(SWIGLU_ALPHA * g) * (u + 1.0)


def moe_ffn(xn, w_router, b_router, w_gate, b_gate, w_up, b_up, w_down, b_down):
    B, S, D = xn.shape
    T = B * S
    xt = xn.reshape(T, D)
    logits = (xt @ w_router + b_router).astype(jnp.float32)
    top_val, top_idx = lax.top_k(logits, TOP_K)
    weights = jax.nn.softmax(top_val, axis=-1)
    A = T * TOP_K
    flat_e = top_idx.reshape(A)
    order = jnp.argsort(flat_e)
    sorted_e = flat_e[order]
    counts = jnp.bincount(flat_e, length=N_EXPERTS)
    padded = (counts + EXPERT_ROW_BLOCK - 1) // EXPERT_ROW_BLOCK * EXPERT_ROW_BLOCK
    start = jnp.cumsum(counts) - counts
    pend = jnp.cumsum(padded)
    pstart = pend - padded
    dest = pstart[sorted_e] + jnp.arange(A) - start[sorted_e]
    P = A + N_EXPERTS * EXPERT_ROW_BLOCK
    n_blocks = P // EXPERT_ROW_BLOCK
    row_token = jnp.full((P,), T, jnp.int32).at[dest].set((order // TOP_K).astype(jnp.int32))
    row_weight = jnp.zeros((P,), jnp.float32).at[dest].set(weights.reshape(A)[order])
    block_start = jnp.arange(n_blocks) * EXPERT_ROW_BLOCK
    block_expert = jnp.minimum(jnp.searchsorted(pend, block_start, side='right'), N_EXPERTS - 1)
    x_pad = jnp.concatenate([xt, jnp.zeros((1, D), xt.dtype)], axis=0)

    def expert_block(args):
        rows, e, wts = args
        xb = x_pad[rows]
        g = xb @ w_gate[e] + b_gate[e]
        u = xb @ w_up[e] + b_up[e]
        y = clamped_swiglu(g, u) @ w_down[e] + b_down[e]
        return y * wts[:, None].astype(y.dtype)

    y = lax.map(expert_block, (row_token.reshape(n_blocks, EXPERT_ROW_BLOCK), block_expert,
                               row_weight.reshape(n_blocks, EXPERT_ROW_BLOCK)))
    out = jnp.zeros((T + 1, D), y.dtype).at[row_token].add(y.reshape(P, D))
    return out[:T].reshape(B, S, D)


def setup_inputs(seed: int = 0) -> dict:
    key = jax.random.key(seed)
    ks = jax.random.split(key, 32)
    f32 = jnp.float32
    L = DEPTH

    def nrm(k, shape, scale):
        return jax.random.normal(k, shape, f32) * scale

    def gain(k, shape):
        return 1.0 + 0.05 * jax.random.normal(k, shape, f32)

    mw = MOBA_HEADS * MOBA_HEAD_DIM
    return {
        'x': nrm(ks[0], (BATCH, SEQ, D_MODEL), 1.0),
        'p': nrm(ks[1], (DEPTH, BATCH, SEQ, PLE_DIM), 1.0),
        'attn_norm': gain(ks[2], (L, D_MODEL)),
        'w_in': nrm(ks[3], (L, D_MODEL, IN_COLS), D_MODEL ** -0.5),
        'q_lat_norm': gain(ks[4], (L, MLA_Q_LORA)),
        'kv_lat_norm': gain(ks[5], (L, MLA_KV_LORA)),
        'w_uq': nrm(ks[6], (L, MLA_Q_LORA, MLA_HEADS * (MLA_NOPE + MLA_ROPE)), MLA_Q_LORA ** -0.5),
        'w_ukv': nrm(ks[7], (L, MLA_KV_LORA, MLA_HEADS * (MLA_NOPE + MLA_V)), MLA_KV_LORA ** -0.5),
        'w_o_mla': nrm(ks[8], (L, MLA_HEADS * MLA_V, D_MODEL), (MLA_HEADS * MLA_V) ** -0.5),
        'w_o_moba': nrm(ks[9], (L, mw, D_MODEL), mw ** -0.5),
        'w_out': nrm(ks[10], (L, D_MODEL, D_MODEL), D_MODEL ** -0.5),
        'rel_bias': nrm(ks[11], (REL_BUCKETS, MOBA_HEADS), 0.2),
        'moe_norm': gain(ks[12], (L, D_MODEL)),
        'w_router': nrm(ks[13], (L, D_MODEL, N_EXPERTS), D_MODEL ** -0.5),
        'b_router': nrm(ks[14], (L, N_EXPERTS), 0.01),
        'w_gate': nrm(ks[15], (L, N_EXPERTS, D_MODEL, D_EXPERT), D_MODEL ** -0.5),
        'b_gate': nrm(ks[16], (L, N_EXPERTS, D_EXPERT), 0.02),
        'w_up': nrm(ks[17], (L, N_EXPERTS, D_MODEL, D_EXPERT), D_MODEL ** -0.5),
        'b_up': nrm(ks[18], (L, N_EXPERTS, D_EXPERT), 0.02),
        'w_down': nrm(ks[19], (L, N_EXPERTS, D_EXPERT, D_MODEL), D_EXPERT ** -0.5),
        'b_down': nrm(ks[20], (L, N_EXPERTS, D_MODEL), 0.02),
        'ple_norm': gain(ks[21], (L, D_MODEL)),
        'w_ple_gate': nrm(ks[22], (L, D_MODEL, D_MODEL), D_MODEL ** -0.5),
        'w_ple': nrm(ks[23], (L, PLE_DIM, D_MODEL), PLE_DIM ** -0.5),
        'final_norm': gain(ks[24], (D_MODEL,)),
    }


def reference(x, p, attn_norm, w_in, q_lat_norm, kv_lat_norm, w_uq, w_ukv, w_o_mla, w_o_moba,
              w_out, rel_bias, moe_norm, w_router, b_router, w_gate, b_gate, w_up, b_up,
              w_down, b_down, ple_norm, w_ple_gate, w_ple, final_norm):
    B, S, _ = x.shape
    h = x
    for i in range(DEPTH):
        xn = rms_norm(h, attn_norm[i])
        z = xn @ w_in[i]
        q_lat, kv_lat, k_rope, q_b, k_b, v_b, g_a, g_b = split_cols(z, IN_SIZES)
        y_a = mla_attention(q_lat, kv_lat, k_rope, q_lat_norm[i], kv_lat_norm[i], w_uq[i], w_ukv[i])
        hs = (B, S, MOBA_HEADS, MOBA_HEAD_DIM)
        y_b = moba_attention(q_b.reshape(hs), k_b.reshape(hs), v_b.reshape(hs), rel_bias)
        y_b = y_b.reshape(B, S, MOBA_HEADS * MOBA_HEAD_DIM)
        merged = (jax.nn.sigmoid(g_a) * (y_a @ w_o_mla[i])
                  + jax.nn.sigmoid(g_b) * (y_b @ w_o_moba[i]))
        h = h + merged @ w_out[i]
        h = h + moe_ffn(rms_norm(h, moe_norm[i]), w_router[i], b_router[i], w_gate[i], b_gate[i],
                        w_up[i], b_up[i], w_down[i], b_down[i])
        ple_gate = jax.nn.sigmoid(rms_norm(h, ple_norm[i]) @ w_ple_gate[i])
        h = h + (p[i] @ w_ple[i]) * ple_gate
    return rms_norm(h, final_norm)
```

```python
import functools
import math

import jax
import jax.numpy as jnp
from jax import lax
from jax.experimental import pallas as pl
from jax.experimental.pallas import tpu as pltpu

F32 = jnp.float32
BF16 = jnp.bfloat16
I32 = jnp.int32
SDS = jax.ShapeDtypeStruct

EPS = 1e-6
NEG_INF = -1e30
MLA_HEADS = 8
MLA_NOPE = 128
MLA_ROPE = 64
MLA_V = 128
MLA_Q_LORA = 512
MLA_KV_LORA = 512
ROPE_THETA = 10000.0
MOBA_HEADS = 8
MOBA_HEAD_DIM = 128
MOBA_BLOCK = 256
MOBA_TOPK = 3
REL_BUCKETS = 32
REL_MAX_DIST = 128
TOP_K = 4
SWIGLU_LIMIT = 7.0
SWIGLU_ALPHA = 1.702

LANES = 128
MLA_SLOT = 2 * LANES
VMEM_LIMIT = 56 * 2**20

IN_TM, IN_TN = 512, 1024
MLAP_TM = 512
MLA_TQ = 512
OUT_TM = 256
ROUTE_TM = 512
EXPERT_ROWS = 512
EXPERT_TN = 512
DISPATCH_TM = 512
COMBINE_TM = 256

_NT = (((1,), (1,)), ((), ()))


def _params(n_axes):
    return pltpu.CompilerParams(dimension_semantics=("arbitrary",) * n_axes,
                                vmem_limit_bytes=VMEM_LIMIT)


def _rms(x, g):
    return x * lax.rsqrt(jnp.mean(x * x, axis=-1, keepdims=True) + EPS) * g


def _dot(a, b):
    return jnp.dot(a, b, preferred_element_type=F32)


def _in_proj_kernel(x_ref, g_ref, w_ref, wkr_ref, lat_ref, qkv_ref, gates_ref, kr_ref, xn_ref,
                    *, n_lat, n_qkv):
    j = pl.program_id(1)

    @pl.when(j == 0)
    def _():
        xn = _rms(x_ref[...], g_ref[...]).astype(BF16)
        xn_ref[...] = xn
        kr_ref[...] = _dot(xn, wkr_ref[...])

    acc = _dot(xn_ref[...], w_ref[...])

    @pl.when(j < n_lat)
    def _():
        lat_ref[...] = acc

    @pl.when((j >= n_lat) & (j < n_lat + n_qkv))
    def _():
        qkv_ref[...] = acc.astype(BF16)

    @pl.when(j >= n_lat + n_qkv)
    def _():
        gates_ref[...] = acc


def _in_proj(x2, gain, w_main, w_kr, n_lat_cols, n_qkv_cols):
    T, D = x2.shape
    n_g_cols = w_main.shape[1] - n_lat_cols - n_qkv_cols
    tm = min(IN_TM, T)
    tn = math.gcd(math.gcd(IN_TN, n_lat_cols), math.gcd(n_qkv_cols, n_g_cols))
    n_lat, n_qkv, n_g = n_lat_cols // tn, n_qkv_cols // tn, n_g_cols // tn
    return pl.pallas_call(
        functools.partial(_in_proj_kernel, n_lat=n_lat, n_qkv=n_qkv),
        grid=(T // tm, n_lat + n_qkv + n_g),
        in_specs=[pl.BlockSpec((tm, D), lambda i, j: (i, 0)),
                  pl.BlockSpec((1, D), lambda i, j: (0, 0)),
                  pl.BlockSpec((D, tn), lambda i, j: (0, j)),
                  pl.BlockSpec((D, LANES), lambda i, j: (0, 0))],
        out_specs=[pl.BlockSpec((tm, tn), lambda i, j: (i, jnp.minimum(j, n_lat - 1))),
                   pl.BlockSpec((tm, tn), lambda i, j: (i, jnp.clip(j - n_lat, 0, n_qkv - 1))),
                   pl.BlockSpec((tm, tn), lambda i, j: (i, jnp.maximum(j - n_lat - n_qkv, 0))),
                   pl.BlockSpec((tm, LANES), lambda i, j: (i, 0))],
        out_shape=[SDS((T, n_lat_cols), F32), SDS((T, n_qkv_cols), BF16),
                   SDS((T, n_g_cols), F32), SDS((T, LANES), F32)],
        scratch_shapes=[pltpu.VMEM((tm, D), BF16)],
        compiler_params=_params(2),
        name="in_proj",
    )(x2, gain, w_main, w_kr)


def _mla_proj_kernel(lat_ref, kr_ref, cos_ref, sin_ref, qn_ref, kvn_ref, wq_ref, wk_ref, wv_ref,
                     q_ref, k_ref, v_ref):
    lat = lat_ref[...]
    qn = _rms(lat[:, :MLA_Q_LORA], qn_ref[...]).astype(BF16)
    kvn = _rms(lat[:, MLA_Q_LORA:], kvn_ref[...]).astype(BF16)
    q = _dot(qn, wq_ref[...])
    kn = _dot(kvn, wk_ref[...])
    v_ref[...] = _dot(kvn, wv_ref[...]).astype(BF16)
    c = cos_ref[...]
    s = sin_ref[...]
    half = MLA_ROPE // 2
    lane = lax.broadcasted_iota(I32, c.shape, 1)

    def rope(xr):
        swapped = jnp.where(lane < half, pltpu.roll(xr, LANES - half, 1), pltpu.roll(xr, half, 1))
        return xr * c + swapped * s

    kr = rope(kr_ref[...]).astype(BF16)
    for h in range(MLA_HEADS):
        lo = h * MLA_SLOT
        q_ref[:, lo:lo + LANES] = q[:, lo:lo + LANES].astype(BF16)
        q_ref[:, lo + LANES:lo + MLA_SLOT] = rope(q[:, lo + LANES:lo + MLA_SLOT]).astype(BF16)
        k_ref[:, lo:lo + LANES] = kn[:, h * MLA_NOPE:(h + 1) * MLA_NOPE].astype(BF16)
        k_ref[:, lo + LANES:lo + MLA_SLOT] = kr


def _mla_proj(lat, kr, cosw, sinw, qnorm, kvnorm, wq, wk, wv, S):
    T = lat.shape[0]
    tm = min(MLAP_TM, S)
    ns = S // tm
    H = MLA_HEADS
    full = lambda a: pl.BlockSpec(a.shape, lambda i: (0,) * a.ndim)
    return pl.pallas_call(
        _mla_proj_kernel,
        grid=(T // tm,),
        in_specs=[pl.BlockSpec((tm, lat.shape[1]), lambda i: (i, 0)),
                  pl.BlockSpec((tm, LANES), lambda i: (i, 0)),
                  pl.BlockSpec((tm, LANES), lambda i: (i % ns, 0)),
                  pl.BlockSpec((tm, LANES), lambda i: (i % ns, 0)),
                  full(qnorm), full(kvnorm), full(wq), full(wk), full(wv)],
        out_specs=[pl.BlockSpec((tm, H * MLA_SLOT), lambda i: (i, 0)),
                   pl.BlockSpec((tm, H * MLA_SLOT), lambda i: (i, 0)),
                   pl.BlockSpec((tm, H * MLA_V), lambda i: (i, 0))],
        out_shape=[SDS((T, H * MLA_SLOT), BF16), SDS((T, H * MLA_SLOT), BF16),
                   SDS((T, H * MLA_V), BF16)],
        compiler_params=_params(1),
        name="mla_proj",
    )(lat, kr, cosw, sinw, qnorm, kvnorm, wq, wk, wv)


def _mla_attn_kernel(q_ref, k_ref, v_ref, o_ref, *, tq, scale):
    qi = pl.program_id(2)
    q = q_ref[...]

    def chunk(start):
        k = k_ref[pl.ds(start, tq), :]
        v = v_ref[pl.ds(start, tq), :]
        return lax.dot_general(q, k, _NT, preferred_element_type=F32) * scale, v

    s, v = chunk(pl.multiple_of(qi * tq, tq))
    row = lax.broadcasted_iota(I32, s.shape, 0)
    col = lax.broadcasted_iota(I32, s.shape, 1)
    s = jnp.where(col <= row, s, NEG_INF)
    m = jnp.max(s, axis=-1, keepdims=True)
    p = jnp.exp(s - m)
    l = jnp.sum(p, axis=-1, keepdims=True)
    acc = _dot(p.astype(BF16), v)

    def body(c, carry):
        m, l, acc = carry
        s, v = chunk(pl.multiple_of(c * tq, tq))
        m_new = jnp.maximum(m, jnp.max(s, axis=-1, keepdims=True))
        alpha = jnp.exp(m - m_new)
        p = jnp.exp(s - m_new)
        l = alpha * l + jnp.sum(p, axis=-1, keepdims=True)
        acc = alpha * acc + _dot(p.astype(BF16), v)
        return m_new, l, acc

    m, l, acc = lax.fori_loop(0, qi, body, (m, l, acc))
    o_ref[...] = (acc / l).astype(BF16)


def _mla_attn(q, k, v, B, S):
    H = MLA_HEADS
    T = B * S
    tq = min(MLA_TQ, S)
    nq = S // tq
    scale = (MLA_NOPE + MLA_ROPE) ** -0.5
    return pl.pallas_call(
        functools.partial(_mla_attn_kernel, tq=tq, scale=scale),
        grid=(B, H, nq),
        in_specs=[pl.BlockSpec((tq, MLA_SLOT), lambda b, h, i: (b * nq + i, h)),
                  pl.BlockSpec((S, MLA_SLOT), lambda b, h, i: (b, h)),
                  pl.BlockSpec((S, MLA_V), lambda b, h, i: (b, h))],
        out_specs=pl.BlockSpec((tq, MLA_V), lambda b, h, i: (b * nq + i, h)),
        out_shape=SDS((T, H * MLA_V), BF16),
        compiler_params=_params(3),
        name="mla_attn",
    )(q, k, v)


def _t5_bucket(dist):
    n = jnp.maximum(dist, 0)
    max_exact = REL_BUCKETS // 2
    large = max_exact + (jnp.log(jnp.maximum(n, 1).astype(F32) / max_exact)
                         / math.log(REL_MAX_DIST / max_exact)
                         * (REL_BUCKETS - max_exact)).astype(I32)
    large = jnp.minimum(large, REL_BUCKETS - 1)
    return jnp.where(n < max_exact, n, large)


def _moba_bias_kernel(rb_ref, bidx_ref, o_ref):
    h = pl.program_id(0)
    for t in range(2):
        bi = bidx_ref[t]
        val = jnp.zeros(bi.shape, F32)
        for b in range(REL_BUCKETS):
            val = jnp.where(bi == b, rb_ref[b, h], val)
        o_ref[t] = val


def _moba_bias(rel_bias, bidx):
    H = rel_bias.shape[1]
    blk = MOBA_BLOCK
    return pl.pallas_call(
        _moba_bias_kernel,
        grid=(H,),
        in_specs=[pl.BlockSpec(memory_space=pltpu.SMEM),
                  pl.BlockSpec((2, blk, blk), lambda h: (0, 0, 0))],
        out_specs=pl.BlockSpec((None, 2, blk, blk), lambda h: (h, 0, 0, 0)),
        out_shape=SDS((H, 2, blk, blk), F32),
        compiler_params=_params(1),
        name="moba_bias",
    )(rel_bias, bidx)


def _moba_attn_kernel(rb_ref, q_ref, k_ref, v_ref, et_ref, bias_ref, o_ref, ka_ref, km_ref, qa_ref,
                      *, nb, n_sel, scale):
    h = pl.program_id(1)
    i = pl.program_id(2)
    blk = MOBA_BLOCK
    d = MOBA_HEAD_DIM

    @pl.when(i == 0)
    def _():
        ka_ref[:, :d] = k_ref[...]
        ka_ref[:, d:] = et_ref[...]
        km_ref[...] = jnp.zeros(km_ref.shape, F32)
        for n in range(nb):
            kb = k_ref[n * blk:(n + 1) * blk, :].astype(F32)
            km_ref[n:n + 1, :] = jnp.sum(kb, axis=0, keepdims=True) * (1.0 / blk)

    q = q_ref[...]
    gate = lax.dot_general(q, km_ref[...].astype(BF16), _NT, preferred_element_type=F32)
    lane = lax.broadcasted_iota(I32, gate.shape, 1)
    g = jnp.where(lane < i, gate, NEG_INF)
    keep = jnp.full(gate.shape, NEG_INF, F32)
    for _ in range(n_sel):
        mx = jnp.max(g, axis=-1, keepdims=True)
        first = jnp.min(jnp.where(g == mx, lane, LANES), axis=-1, keepdims=True)
        pick = lane == first
        keep = jnp.where(pick & (lane < i), 0.0, keep)
        g = jnp.where(pick, -3.0e38, g)
    keep = jnp.where(lane == i, 0.0, keep)
    qa_ref[:, :d] = q
    qa_ref[:, d:] = keep.astype(BF16)
    qa = qa_ref[...]

    def block(n):
        start = pl.multiple_of(n * blk, blk)
        kk = ka_ref[pl.ds(start, blk), :]
        vv = v_ref[pl.ds(start, blk), :]
        return lax.dot_general(qa, kk, _NT, preferred_element_type=F32) * scale, vv

    s, vv = block(i)
    s = s + bias_ref[0]
    row = lax.broadcasted_iota(I32, s.shape, 0)
    col = lax.broadcasted_iota(I32, s.shape, 1)
    s = jnp.where(col <= row, s, NEG_INF)
    m = jnp.max(s, axis=-1, keepdims=True)
    p = jnp.exp(s - m)
    l = jnp.sum(p, axis=-1, keepdims=True)
    acc = _dot(p.astype(BF16), vv)

    def online(carry, s, vv):
        m, l, acc = carry
        m_new = jnp.maximum(m, jnp.max(s, axis=-1, keepdims=True))
        alpha = jnp.exp(m - m_new)
        p = jnp.exp(s - m_new)
        l = alpha * l + jnp.sum(p, axis=-1, keepdims=True)
        acc = alpha * acc + _dot(p.astype(BF16), vv)
        return m_new, l, acc

    def adjacent(n, carry):
        s, vv = block(n)
        return online(carry, s + bias_ref[1], vv)

    far_bias = rb_ref[REL_BUCKETS - 1, h]

    def far(n, carry):
        s, vv = block(n)
        return online(carry, s + far_bias, vv)

    n_far = jnp.maximum(i - 1, 0)
    carry = lax.fori_loop(n_far, i, adjacent, (m, l, acc))
    m, l, acc = lax.fori_loop(0, n_far, far, carry)
    o_ref[...] = (acc / l).astype(BF16)


def _moba_attn(qkv, et, bias, rel_bias, B, S):
    H, d, blk = MOBA_HEADS, MOBA_HEAD_DIM, MOBA_BLOCK
    T = B * S
    nb = S // blk
    n_sel = min(MOBA_TOPK, nb)
    return pl.pallas_call(
        functools.partial(_moba_attn_kernel, nb=nb, n_sel=n_sel, scale=d ** -0.5),
        grid=(B, H, nb),
        in_specs=[pl.BlockSpec(memory_space=pltpu.SMEM),
                  pl.BlockSpec((blk, d), lambda b, h, i: (b * nb + i, h)),
                  pl.BlockSpec((S, d), lambda b, h, i: (b, H + h)),
                  pl.BlockSpec((S, d), lambda b, h, i: (b, 2 * H + h)),
                  pl.BlockSpec((S, LANES), lambda b, h, i: (0, 0)),
                  pl.BlockSpec((None, 2, blk, blk), lambda b, h, i: (h, 0, 0, 0))],
        out_specs=pl.BlockSpec((blk, d), lambda b, h, i: (b * nb + i, h)),
        out_shape=SDS((T, H * d), BF16),
        scratch_shapes=[pltpu.VMEM((S, d + LANES), BF16),
                        pltpu.VMEM((LANES, d), F32),
                        pltpu.VMEM((blk, d + LANES), BF16)],
        compiler_params=_params(3),
        name="moba_attn",
    )(rel_bias, qkv, qkv, qkv, et, bias)


def _out_proj_kernel(ya_ref, yb_ref, g_ref, x_ref, woa_ref, wob_ref, wout_ref, mg_ref, wr_ref, br_ref,
                     h1_ref, xn_ref, lg_ref, *, D):
    a = _dot(ya_ref[...], woa_ref[...])
    b = _dot(yb_ref[...], wob_ref[...])
    g = g_ref[...]
    merged = jax.nn.sigmoid(g[:, :D]) * a + jax.nn.sigmoid(g[:, D:]) * b
    h1 = x_ref[...] + _dot(merged.astype(BF16), wout_ref[...])
    h1_ref[...] = h1
    xn = _rms(h1, mg_ref[...])
    xn_ref[...] = xn
    lg_ref[...] = _dot(xn.astype(BF16), wr_ref[...]) + br_ref[...]


def _resident(a):
    return pl.BlockSpec(a.shape, lambda i: (0,) * a.ndim, pipeline_mode=pl.Buffered(1))


def _out_proj(ya, yb, gates, x2, woa, wob, wout, mgain, wr, br):
    T, D = x2.shape
    tm = min(OUT_TM, T)
    row = lambda a: pl.BlockSpec((tm, a.shape[1]), lambda i: (i, 0))
    return pl.pallas_call(
        functools.partial(_out_proj_kernel, D=D),
        grid=(T // tm,),
        in_specs=[row(ya), row(yb), row(gates), row(x2),
                  _resident(woa), _resident(wob), _resident(wout), _resident(mgain),
                  _resident(wr), _resident(br)],
        out_specs=[pl.BlockSpec((tm, D), lambda i: (i, 0)),
                   pl.BlockSpec((tm, D), lambda i: (i, 0)),
                   pl.BlockSpec((tm, LANES), lambda i: (i, 0))],
        out_shape=[SDS((T, D), F32), SDS((T, D), F32), SDS((T, LANES), F32)],
        compiler_params=_params(1),
        name="out_proj",
    )(ya, yb, gates, x2, woa, wob, wout, mgain, wr, br)


def _route_kernel(lg_ref, e_ref, w_ref, r_ref, cnt_ref, carry_ref, *, n_exp):
    i = pl.program_id(0)

    @pl.when(i == 0)
    def _():
        carry_ref[...] = jnp.zeros(carry_ref.shape, F32)

    lt = lg_ref[...].T
    tm = lt.shape[1]
    sub = lax.broadcasted_iota(I32, lt.shape, 0)
    neg = -jnp.inf
    cur = jnp.where(sub < n_exp, lt, neg)
    vals, idxs = [], []
    for _ in range(TOP_K):
        mx = jnp.max(cur, axis=0, keepdims=True)
        ix = jnp.min(jnp.where(cur == mx, sub, LANES), axis=0, keepdims=True)
        vals.append(mx)
        idxs.append(ix)
        cur = jnp.where(sub == ix, neg, cur)
    ex = [jnp.exp(v - vals[0]) for v in vals]
    den = ex[0]
    for e in ex[1:]:
        den = den + e
    onehot = jnp.zeros(lt.shape, F32)
    for ix in idxs:
        onehot = onehot + (sub == ix).astype(F32)
    r_i = lax.broadcasted_iota(I32, (tm, tm), 0)
    c_i = lax.broadcasted_iota(I32, (tm, tm), 1)
    tri = (r_i <= c_i).astype(BF16)
    incl = _dot(onehot.astype(BF16), tri)
    base = carry_ref[...]
    excl = incl - onehot + base[:, :1]
    e_ref[...] = jnp.zeros(e_ref.shape, I32)
    w_ref[...] = jnp.zeros(w_ref.shape, F32)
    r_ref[...] = jnp.zeros(r_ref.shape, I32)
    for k in range(TOP_K):
        e_ref[k:k + 1, :] = idxs[k]
        w_ref[k:k + 1, :] = ex[k] / den
        rk = jnp.sum(jnp.where(sub == idxs[k], excl, 0.0), axis=0, keepdims=True)
        r_ref[k:k + 1, :] = rk.astype(I32)
    total = base + jnp.sum(onehot, axis=1, keepdims=True)
    carry_ref[...] = total
    cnt_ref[...] = total


def _route(logits, n_exp):
    T = logits.shape[0]
    tm = min(ROUTE_TM, T)
    return pl.pallas_call(
        functools.partial(_route_kernel, n_exp=n_exp),
        grid=(T // tm,),
        in_specs=[pl.BlockSpec((tm, LANES), lambda i: (i, 0))],
        out_specs=[pl.BlockSpec((8, tm), lambda i: (0, i)),
                   pl.BlockSpec((8, tm), lambda i: (0, i)),
                   pl.BlockSpec((8, tm), lambda i: (0, i)),
                   pl.BlockSpec((LANES, LANES), lambda i: (0, 0))],
        out_shape=[SDS((8, T), I32), SDS((8, T), F32), SDS((8, T), I32), SDS((LANES, LANES), F32)],
        scratch_shapes=[pltpu.VMEM((LANES, LANES), F32)],
        compiler_params=_params(1),
        name="route",
    )(logits)


def _dispatch_kernel(zt_ref, nz_ref, dest_ref, x_hbm, xg_hbm, zbuf_ref, sem, *, tm, rows):
    i = pl.program_id(0)

    @pl.when(i == 0)
    def _():
        zbuf_ref[...] = jnp.zeros(zbuf_ref.shape, zbuf_ref.dtype)

        def zcopy(j):
            start = pl.multiple_of(zt_ref[j] * rows, rows)
            return pltpu.make_async_copy(zbuf_ref, xg_hbm.at[pl.ds(start, rows)], sem)

        def zstart(j, c):
            zcopy(j).start()
            return c

        def zwait(j, c):
            zcopy(j).wait()
            return c

        lax.fori_loop(0, nz_ref[0], zstart, 0)
        lax.fori_loop(0, nz_ref[0], zwait, 0)

    def copy(t, k):
        return pltpu.make_async_copy(x_hbm.at[pl.ds(i * tm + t, 1)],
                                     xg_hbm.at[pl.ds(dest_ref[k, t], 1)], sem)

    def start(t, c):
        for k in range(TOP_K):
            copy(t, k).start()
        return c

    def wait(t, c):
        for k in range(TOP_K):
            copy(t, k).wait()
        return c

    lax.fori_loop(0, tm, start, 0)
    lax.fori_loop(0, tm, wait, 0)


def _dispatch(ztiles, nz, dest, xn, n_rows):
    T, D = xn.shape
    tm = min(DISPATCH_TM, T)
    return pl.pallas_call(
        functools.partial(_dispatch_kernel, tm=tm, rows=EXPERT_ROWS),
        grid_spec=pltpu.PrefetchScalarGridSpec(
            num_scalar_prefetch=2,
            grid=(T // tm,),
            in_specs=[pl.BlockSpec((8, tm), lambda i, zt, nz: (0, i), memory_space=pltpu.SMEM),
                      pl.BlockSpec(memory_space=pl.ANY)],
            out_specs=pl.BlockSpec(memory_space=pl.ANY),
            scratch_shapes=[pltpu.VMEM((EXPERT_ROWS, D), F32), pltpu.SemaphoreType.DMA]),
        out_shape=SDS((n_rows, D), F32),
        compiler_params=_params(1),
        name="dispatch",
    )(ztiles, nz, dest, xn)


def _ffn_up_kernel(te_ref, nu_ref, x_ref, wg_ref, wu_ref, bg_ref, bu_ref, h_ref, wgb_ref, wub_ref):
    i = pl.program_id(1)
    used = i < nu_ref[0]
    new_expert = (i == 0) | (te_ref[i] != te_ref[jnp.maximum(i - 1, 0)])

    @pl.when(used & new_expert)
    def _():
        wgb_ref[...] = wg_ref[...].astype(BF16)
        wub_ref[...] = wu_ref[...].astype(BF16)

    @pl.when(used)
    def _():
        x = x_ref[...].astype(BF16)
        g = _dot(x, wgb_ref[...]) + bg_ref[...]
        u = _dot(x, wub_ref[...]) + bu_ref[...]
        g = jnp.minimum(g, SWIGLU_LIMIT)
        u = jnp.clip(u, -SWIGLU_LIMIT, SWIGLU_LIMIT)
        h_ref[...] = (g * jax.nn.sigmoid(SWIGLU_ALPHA * g) * (u + 1.0)).astype(BF16)

    @pl.when(jnp.logical_not(used))
    def _():
        h_ref[...] = jnp.zeros(h_ref.shape, h_ref.dtype)


def _ffn_up(tile_expert, n_used, xg, wg, wu, bg, bu):
    P, D = xg.shape
    Dx = wg.shape[2]
    tr = EXPERT_ROWS
    tn = min(EXPERT_TN, Dx)
    n_tiles = P // tr
    xmap = lambda j, i, te, nu: (jnp.minimum(i, nu[0] - 1), 0)
    wmap = lambda j, i, te, nu: (te[i], 0, j)
    return pl.pallas_call(
        _ffn_up_kernel,
        grid_spec=pltpu.PrefetchScalarGridSpec(
            num_scalar_prefetch=2,
            grid=(Dx // tn, n_tiles),
            in_specs=[pl.BlockSpec((tr, D), xmap),
                      pl.BlockSpec((None, D, tn), wmap),
                      pl.BlockSpec((None, D, tn), wmap),
                      pl.BlockSpec((None, 1, tn), wmap),
                      pl.BlockSpec((None, 1, tn), wmap)],
            out_specs=pl.BlockSpec((tr, tn), lambda j, i, te, nu: (i, j)),
            scratch_shapes=[pltpu.VMEM((D, tn), BF16), pltpu.VMEM((D, tn), BF16)]),
        out_shape=SDS((P, Dx), BF16),
        compiler_params=_params(2),
        name="ffn_up",
    )(tile_expert, n_used, xg, wg, wu, bg, bu)


def _ffn_down_kernel(te_ref, nu_ref, h_ref, wd_ref, bd_ref, y_ref, wdb_ref):
    i = pl.program_id(1)
    used = i < nu_ref[0]
    new_expert = (i == 0) | (te_ref[i] != te_ref[jnp.maximum(i - 1, 0)])

    @pl.when(used & new_expert)
    def _():
        wdb_ref[...] = wd_ref[...].astype(BF16)

    @pl.when(used)
    def _():
        y_ref[...] = _dot(h_ref[...], wdb_ref[...]) + bd_ref[...]

    @pl.when(jnp.logical_not(used))
    def _():
        y_ref[...] = jnp.zeros(y_ref.shape, y_ref.dtype)


def _ffn_down(tile_expert, n_used, hid, wd, bd):
    P, Dx = hid.shape
    D = wd.shape[2]
    tr = EXPERT_ROWS
    tn = min(EXPERT_TN, D)
    n_tiles = P // tr
    hmap = lambda j, i, te, nu: (jnp.minimum(i, nu[0] - 1), 0)
    wmap = lambda j, i, te, nu: (te[i], 0, j)
    return pl.pallas_call(
        _ffn_down_kernel,
        grid_spec=pltpu.PrefetchScalarGridSpec(
            num_scalar_prefetch=2,
            grid=(D // tn, n_tiles),
            in_specs=[pl.BlockSpec((tr, Dx), hmap),
                      pl.BlockSpec((None, Dx, tn), wmap),
                      pl.BlockSpec((None, 1, tn), wmap)],
            out_specs=pl.BlockSpec((tr, tn), lambda j, i, te, nu: (i, j)),
            scratch_shapes=[pltpu.VMEM((Dx, tn), BF16)]),
        out_shape=SDS((P, D), F32),
        compiler_params=_params(2),
        name="ffn_down",
    )(tile_expert, n_used, hid, wd, bd)


def _combine_kernel(dest_ref, w_ref, h1_ref, p_ref, wple_ref, wpg_ref, pg_ref, fg_ref, y_hbm,
                    o_ref, ybuf_ref, sem, *, tm, final):
    def copy(t, k):
        return pltpu.make_async_copy(y_hbm.at[pl.ds(dest_ref[k, t], 1)],
                                     ybuf_ref.at[k, pl.ds(t, 1)], sem)

    def start(t, c):
        for k in range(TOP_K):
            copy(t, k).start()
        return c

    def wait(t, c):
        for k in range(TOP_K):
            copy(t, k).wait()
        return c

    lax.fori_loop(0, tm, start, 0)
    pw = _dot(p_ref[...].astype(BF16), wple_ref[...])
    lax.fori_loop(0, tm, wait, 0)
    w = w_ref[...]
    moe = w[:, 0:1] * ybuf_ref[0]
    for k in range(1, TOP_K):
        moe = moe + w[:, k:k + 1] * ybuf_ref[k]
    h2 = h1_ref[...] + moe
    xn = _rms(h2, pg_ref[...]).astype(BF16)
    gate = jax.nn.sigmoid(_dot(xn, wpg_ref[...]))
    h3 = h2 + pw * gate
    o_ref[...] = _rms(h3, fg_ref[...]) if final else h3


def _combine(dest, wts, h1, p2, wple, wpg, pgain, fgain, yg, final):
    T, D = h1.shape
    tm = min(COMBINE_TM, T)
    row = lambda a: pl.BlockSpec((tm, a.shape[1]), lambda i: (i, 0))
    return pl.pallas_call(
        functools.partial(_combine_kernel, tm=tm, final=final),
        grid=(T // tm,),
        in_specs=[pl.BlockSpec((8, tm), lambda i: (0, i), memory_space=pltpu.SMEM),
                  row(wts), row(h1), row(p2),
                  _resident(wple), _resident(wpg), _resident(pgain), _resident(fgain),
                  pl.BlockSpec(memory_space=pl.ANY)],
        out_specs=pl.BlockSpec((tm, D), lambda i: (i, 0)),
        out_shape=SDS((T, D), F32),
        scratch_shapes=[pltpu.VMEM((TOP_K, tm, D), F32), pltpu.SemaphoreType.DMA],
        compiler_params=_params(1),
        name="combine",
    )(dest, wts, h1, p2, wple, wpg, pgain, fgain, yg)


def kernel(x, p, attn_norm, w_in, q_lat_norm, kv_lat_norm, w_uq, w_ukv, w_o_mla, w_o_moba, w_out,
           rel_bias, moe_norm, w_router, b_router, w_gate, b_gate, w_up, b_up, w_down, b_down,
           ple_norm, w_ple_gate, w_ple, final_norm):
    B, S, D = x.shape
    T = B * S
    n_layers = w_in.shape[0]
    E = w_router.shape[-1]
    H = MLA_HEADS
    mw = MOBA_HEADS * MOBA_HEAD_DIM
    assert S % MOBA_BLOCK == 0 and E <= LANES
    o_kr = MLA_Q_LORA + MLA_KV_LORA
    o_q = o_kr + MLA_ROPE
    o_g = o_q + 3 * mw

    inv = 1.0 / (ROPE_THETA ** (jnp.arange(0, MLA_ROPE, 2, dtype=F32) / MLA_ROPE))
    ang = jnp.arange(S, dtype=F32)[:, None] * inv[None, :]
    cos, sin = jnp.cos(ang), jnp.sin(ang)
    zpad = jnp.zeros((S, LANES - MLA_ROPE), F32)
    cosw = jnp.concatenate([cos, cos, zpad], axis=1)
    sinw = jnp.concatenate([-sin, sin, zpad], axis=1)
    r = jnp.arange(MOBA_BLOCK)
    d0 = r[:, None] - r[None, :]
    bidx = jnp.stack([_t5_bucket(d0), _t5_bucket(d0 + MOBA_BLOCK)]).astype(I32)
    et = (jnp.arange(S)[:, None] // MOBA_BLOCK == jnp.arange(LANES)[None, :]).astype(BF16)
    bias = _moba_bias(rel_bias, bidx)

    h = x.reshape(T, D)
    for li in range(n_layers):
        w = w_in[li]
        w_main = jnp.concatenate([w[:, :o_kr], w[:, o_q:]], axis=1).astype(BF16)
        w_kr = jnp.pad(w[:, o_kr:o_q], ((0, 0), (0, LANES - MLA_ROPE))).astype(BF16)
        wq = w_uq[li].reshape(MLA_Q_LORA, H, MLA_NOPE + MLA_ROPE)
        wq = jnp.pad(wq, ((0, 0), (0, 0), (0, MLA_SLOT - MLA_NOPE - MLA_ROPE)))
        wq = wq.reshape(MLA_Q_LORA, H * MLA_SLOT).astype(BF16)
        wkv = w_ukv[li].reshape(MLA_KV_LORA, H, MLA_NOPE + MLA_V)
        wk = wkv[:, :, :MLA_NOPE].reshape(MLA_KV_LORA, H * MLA_NOPE).astype(BF16)
        wv = wkv[:, :, MLA_NOPE:].reshape(MLA_KV_LORA, H * MLA_V).astype(BF16)
        wr = jnp.pad(w_router[li], ((0, 0), (0, LANES - E))).astype(BF16)
        br = jnp.pad(b_router[li], (0, LANES - E), constant_values=NEG_INF)[None, :]

        lat, qkv, gates, kr = _in_proj(h, attn_norm[li][None, :], w_main, w_kr, o_kr, 3 * mw)
        q_a, k_a, v_a = _mla_proj(lat, kr, cosw, sinw, q_lat_norm[li][None, :], kv_lat_norm[li][None, :],
                                  wq, wk, wv, S)
        y_a = _mla_attn(q_a, k_a, v_a, B, S)
        y_b = _moba_attn(qkv, et, bias, rel_bias, B, S)
        h1, xn, logits = _out_proj(y_a, y_b, gates, h, w_o_mla[li].astype(BF16), w_o_moba[li].astype(BF16),
                                   w_out[li].astype(BF16), moe_norm[li][None, :], wr, br)

        e_k, w_k, r_k, cnt = _route(logits, E)
        counts = cnt[:E, 0].astype(I32)
        tiles = (counts + EXPERT_ROWS - 1) // EXPERT_ROWS
        tile_end = jnp.cumsum(tiles)
        tile_start = tile_end - tiles
        n_tiles = (T * TOP_K) // EXPERT_ROWS + E
        n_used = tile_end[-1]
        tile_ids = jnp.arange(n_tiles)
        tile_expert = jnp.minimum(jnp.searchsorted(tile_end, jnp.minimum(tile_ids, n_used - 1), side='right'),
                                  E - 1).astype(I32)
        dest = (tile_start * EXPERT_ROWS)[e_k] + r_k
        is_last = jnp.zeros((n_tiles,), jnp.bool_).at[jnp.where(tiles > 0, tile_end - 1, n_tiles)].set(
            True, mode='drop')
        needs_zero = is_last | (tile_ids >= n_used)
        ztiles = jnp.nonzero(needs_zero, size=n_tiles, fill_value=0)[0].astype(I32)
        nz = jnp.sum(needs_zero).astype(I32)[None]
        nu = n_used.astype(I32)[None]

        xg = _dispatch(ztiles, nz, dest, xn, n_tiles * EXPERT_ROWS)
        hid = _ffn_up(tile_expert, nu, xg, w_gate[li], w_up[li], b_gate[li][:, None, :], b_up[li][:, None, :])
        yg = _ffn_down(tile_expert, nu, hid, w_down[li], b_down[li][:, None, :])
        h = _combine(dest, w_k[:TOP_K].T, h1, p[li].reshape(T, -1), w_ple[li].astype(BF16),
                     w_ple_gate[li].astype(BF16), ple_norm[li][None, :], final_norm[None, :], yg,
                     final=li == n_layers - 1)
    return h.reshape(B, S, D)
```

```python
import functools
import math

import jax
import jax.numpy as jnp
from jax import lax
from jax.experimental import pallas as pl
from jax.experimental.pallas import tpu as pltpu

F32 = jnp.float32
BF16 = jnp.bfloat16
I32 = jnp.int32
SDS = jax.ShapeDtypeStruct

EPS = 1e-6
NEG_INF = -1e30
MLA_HEADS = 8
MLA_NOPE = 128
MLA_ROPE = 64
MLA_V = 128
MLA_Q_LORA = 512
MLA_KV_LORA = 512
ROPE_THETA = 10000.0
MOBA_HEADS = 8
MOBA_HEAD_DIM = 128
MOBA_BLOCK = 256
MOBA_TOPK = 3
REL_BUCKETS = 32
REL_MAX_DIST = 128
TOP_K = 4
SWIGLU_LIMIT = 7.0
SWIGLU_ALPHA = 1.702

LANES = 128
MLA_SLOT = 2 * LANES
VMEM_LIMIT = 56 * 2**20

IN_TM, IN_TN = 512, 1024
MLAP_TM = 512
MLA_TQ = 512
MLA_TK = 512
OUT_TM = 256
ROUTE_TM = 512
EXPERT_ROWS = 512
EXPERT_TN = 512
DISPATCH_TM = 512
DEST_TB = 2048
COMBINE_TM = 256

_NT = (((1,), (1,)), ((), ()))


def _params(n_axes):
    return pltpu.CompilerParams(dimension_semantics=("arbitrary",) * n_axes,
                                vmem_limit_bytes=VMEM_LIMIT)


def _rms(x, g):
    return x * lax.rsqrt(jnp.mean(x * x, axis=-1, keepdims=True) + EPS) * g


def _dot(a, b):
    return jnp.dot(a, b, preferred_element_type=F32)


def _in_proj_kernel(x_ref, g_ref, w_ref, wkr_ref, lat_ref, qkv_ref, gates_ref, kr_ref, xn_ref,
                    *, n_lat, n_qkv):
    j = pl.program_id(1)

    @pl.when(j == 0)
    def _():
        xn = _rms(x_ref[...], g_ref[...]).astype(BF16)
        xn_ref[...] = xn
        kr_ref[...] = _dot(xn, wkr_ref[...])

    acc = _dot(xn_ref[...], w_ref[...])

    @pl.when(j < n_lat)
    def _():
        lat_ref[...] = acc

    @pl.when((j >= n_lat) & (j < n_lat + n_qkv))
    def _():
        qkv_ref[...] = acc.astype(BF16)

    @pl.when(j >= n_lat + n_qkv)
    def _():
        gates_ref[...] = acc


def _in_proj(x2, gain, w_main, w_kr, n_lat_cols, n_qkv_cols):
    T, D = x2.shape
    n_g_cols = w_main.shape[1] - n_lat_cols - n_qkv_cols
    tm = min(IN_TM, T)
    tn = math.gcd(math.gcd(IN_TN, n_lat_cols), math.gcd(n_qkv_cols, n_g_cols))
    n_lat, n_qkv, n_g = n_lat_cols // tn, n_qkv_cols // tn, n_g_cols // tn
    return pl.pallas_call(
        functools.partial(_in_proj_kernel, n_lat=n_lat, n_qkv=n_qkv),
        grid=(T // tm, n_lat + n_qkv + n_g),
        in_specs=[pl.BlockSpec((tm, D), lambda i, j: (i, 0)),
                  pl.BlockSpec((1, D), lambda i, j: (0, 0)),
                  pl.BlockSpec((D, tn), lambda i, j: (0, j)),
                  pl.BlockSpec((D, LANES), lambda i, j: (0, 0))],
        out_specs=[pl.BlockSpec((tm, tn), lambda i, j: (i, jnp.minimum(j, n_lat - 1))),
                   pl.BlockSpec((tm, tn), lambda i, j: (i, jnp.clip(j - n_lat, 0, n_qkv - 1))),
                   pl.BlockSpec((tm, tn), lambda i, j: (i, jnp.maximum(j - n_lat - n_qkv, 0))),
                   pl.BlockSpec((tm, LANES), lambda i, j: (i, 0))],
        out_shape=[SDS((T, n_lat_cols), F32), SDS((T, n_qkv_cols), BF16),
                   SDS((T, n_g_cols), F32), SDS((T, LANES), F32)],
        scratch_shapes=[pltpu.VMEM((tm, D), BF16)],
        compiler_params=_params(2),
        name="in_proj",
    )(x2, gain, w_main, w_kr)


def _mla_proj_kernel(lat_ref, kr_ref, cos_ref, sin_ref, qn_ref, kvn_ref, wq_ref, wk_ref, wv_ref,
                     q_ref, k_ref, v_ref):
    lat = lat_ref[...]
    qn = _rms(lat[:, :MLA_Q_LORA], qn_ref[...]).astype(BF16)
    kvn = _rms(lat[:, MLA_Q_LORA:], kvn_ref[...]).astype(BF16)
    q = _dot(qn, wq_ref[...])
    kn = _dot(kvn, wk_ref[...])
    v_ref[...] = _dot(kvn, wv_ref[...]).astype(BF16)
    c = cos_ref[...]
    s = sin_ref[...]
    half = MLA_ROPE // 2
    lane = lax.broadcasted_iota(I32, c.shape, 1)

    def rope(xr):
        swapped = jnp.where(lane < half, pltpu.roll(xr, LANES - half, 1), pltpu.roll(xr, half, 1))
        return xr * c + swapped * s

    kr = rope(kr_ref[...]).astype(BF16)
    for h in range(MLA_HEADS):
        lo = h * MLA_SLOT
        q_ref[:, lo:lo + LANES] = q[:, lo:lo + LANES].astype(BF16)
        q_ref[:, lo + LANES:lo + MLA_SLOT] = rope(q[:, lo + LANES:lo + MLA_SLOT]).astype(BF16)
        k_ref[:, lo:lo + LANES] = kn[:, h * MLA_NOPE:(h + 1) * MLA_NOPE].astype(BF16)
        k_ref[:, lo + LANES:lo + MLA_SLOT] = kr


def _mla_proj(lat, kr, cosw, sinw, qnorm, kvnorm, wq, wk, wv, S):
    T = lat.shape[0]
    tm = min(MLAP_TM, S)
    ns = S // tm
    H = MLA_HEADS
    full = lambda a: pl.BlockSpec(a.shape, lambda i: (0,) * a.ndim)
    return pl.pallas_call(
        _mla_proj_kernel,
        grid=(T // tm,),
        in_specs=[pl.BlockSpec((tm, lat.shape[1]), lambda i: (i, 0)),
                  pl.BlockSpec((tm, LANES), lambda i: (i, 0)),
                  pl.BlockSpec((tm, LANES), lambda i: (i % ns, 0)),
                  pl.BlockSpec((tm, LANES), lambda i: (i % ns, 0)),
                  full(qnorm), full(kvnorm), full(wq), full(wk), full(wv)],
        out_specs=[pl.BlockSpec((tm, H * MLA_SLOT), lambda i: (i, 0)),
                   pl.BlockSpec((tm, H * MLA_SLOT), lambda i: (i, 0)),
                   pl.BlockSpec((tm, H * MLA_V), lambda i: (i, 0))],
        out_shape=[SDS((T, H * MLA_SLOT), BF16), SDS((T, H * MLA_SLOT), BF16),
                   SDS((T, H * MLA_V), BF16)],
        compiler_params=_params(1),
        name="mla_proj",
    )(lat, kr, cosw, sinw, qnorm, kvnorm, wq, wk, wv)


def _transpose_chunks(src_ref, dst_ref):
    n, _, tk = dst_ref.shape
    for c in range(n):
        dst_ref[c] = src_ref[c * tk:(c + 1) * tk, :].astype(F32).T.astype(dst_ref.dtype)


def _online_softmax_t(carry, s, vt):
    m, l, acc = carry
    m_new = jnp.maximum(m, jnp.max(s, axis=0, keepdims=True))
    alpha = jnp.exp(m - m_new)
    p = jnp.exp(s - m_new)
    l = alpha * l + jnp.sum(p, axis=0, keepdims=True)
    acc = alpha * acc + _dot(vt, p.astype(BF16))
    return m_new, l, acc


def _mla_attn_kernel(q_ref, k_ref, v_ref, o_ref, vt_ref, *, tq, tk, scale):
    qi = pl.program_id(2)

    @pl.when(qi == 0)
    def _():
        _transpose_chunks(v_ref, vt_ref)

    q = q_ref[...]
    per_q = tq // tk

    def chunk(c):
        k = k_ref[pl.ds(pl.multiple_of(c * tk, tk), tk), :]
        return lax.dot_general(k, q, _NT, preferred_element_type=F32) * scale, vt_ref[c]

    carry = None
    for c in range(per_q):
        s, vt = chunk(qi * per_q + c)
        key = lax.broadcasted_iota(I32, s.shape, 0) + c * tk
        qry = lax.broadcasted_iota(I32, s.shape, 1)
        s = jnp.where(key <= qry, s, NEG_INF)
        if carry is None:
            m = jnp.max(s, axis=0, keepdims=True)
            p = jnp.exp(s - m)
            carry = (m, jnp.sum(p, axis=0, keepdims=True), _dot(vt, p.astype(BF16)))
        else:
            carry = _online_softmax_t(carry, s, vt)

    def body(c, carry):
        s, vt = chunk(c)
        return _online_softmax_t(carry, s, vt)

    m, l, acc = lax.fori_loop(0, qi * per_q, body, carry)
    o_ref[...] = (acc / l).T.astype(BF16)


def _mla_attn(q, k, v, B, S):
    H = MLA_HEADS
    T = B * S
    tq = min(MLA_TQ, S)
    tk = min(MLA_TK, tq)
    nq = S // tq
    scale = (MLA_NOPE + MLA_ROPE) ** -0.5
    return pl.pallas_call(
        functools.partial(_mla_attn_kernel, tq=tq, tk=tk, scale=scale),
        grid=(B, H, nq),
        in_specs=[pl.BlockSpec((tq, MLA_SLOT), lambda b, h, i: (b * nq + i, h)),
                  pl.BlockSpec((S, MLA_SLOT), lambda b, h, i: (b, h)),
                  pl.BlockSpec((S, MLA_V), lambda b, h, i: (b, h))],
        out_specs=pl.BlockSpec((tq, MLA_V), lambda b, h, i: (b * nq + i, h)),
        out_shape=SDS((T, H * MLA_V), BF16),
        scratch_shapes=[pltpu.VMEM((S // tk, MLA_V, tk), BF16)],
        compiler_params=_params(3),
        name="mla_attn",
    )(q, k, v)


def _t5_bucket(dist):
    n = jnp.maximum(dist, 0)
    max_exact = REL_BUCKETS // 2
    large = max_exact + (jnp.log(jnp.maximum(n, 1).astype(F32) / max_exact)
                         / math.log(REL_MAX_DIST / max_exact)
                         * (REL_BUCKETS - max_exact)).astype(I32)
    large = jnp.minimum(large, REL_BUCKETS - 1)
    return jnp.where(n < max_exact, n, large)


def _moba_bias_kernel(rb_ref, bidx_ref, o_ref):
    h = pl.program_id(0)
    for t in range(2):
        bi = bidx_ref[t]
        val = jnp.zeros(bi.shape, F32)
        for b in range(REL_BUCKETS):
            val = jnp.where(bi == b, rb_ref[b, h], val)
        o_ref[t] = val


def _moba_bias(rel_bias, bidx):
    H = rel_bias.shape[1]
    blk = MOBA_BLOCK
    return pl.pallas_call(
        _moba_bias_kernel,
        grid=(H,),
        in_specs=[pl.BlockSpec(memory_space=pltpu.SMEM),
                  pl.BlockSpec((2, blk, blk), lambda h: (0, 0, 0))],
        out_specs=pl.BlockSpec((None, 2, blk, blk), lambda h: (h, 0, 0, 0)),
        out_shape=SDS((H, 2, blk, blk), F32),
        compiler_params=_params(1),
        name="moba_bias",
    )(rel_bias, bidx)


def _moba_attn_kernel(rb_ref, q_ref, k_ref, v_ref, et_ref, bias_ref, o_ref, ka_ref, km_ref, qa_ref, vt_ref,
                      *, nb, n_sel, scale):
    h = pl.program_id(1)
    i = pl.program_id(2)
    blk = MOBA_BLOCK
    d = MOBA_HEAD_DIM

    @pl.when(i == 0)
    def _():
        ka_ref[:, :d] = k_ref[...]
        ka_ref[:, d:] = et_ref[...]
        _transpose_chunks(v_ref, vt_ref)
        km_ref[...] = jnp.zeros(km_ref.shape, F32)
        for n in range(nb):
            kb = k_ref[n * blk:(n + 1) * blk, :].astype(F32)
            km_ref[n:n + 1, :] = jnp.sum(kb, axis=0, keepdims=True) * (1.0 / blk)

    q = q_ref[...]
    gate = lax.dot_general(km_ref[...].astype(BF16), q, _NT, preferred_element_type=F32)
    sub = lax.broadcasted_iota(I32, gate.shape, 0)
    g = jnp.where(sub < i, gate, NEG_INF)
    keep = jnp.full(gate.shape, NEG_INF, F32)
    for _ in range(n_sel):
        mx = jnp.max(g, axis=0, keepdims=True)
        first = jnp.min(jnp.where(g == mx, sub, LANES), axis=0, keepdims=True)
        pick = sub == first
        keep = jnp.where(pick & (sub < i), 0.0, keep)
        g = jnp.where(pick, -3.0e38, g)
    keep = jnp.where(sub == i, 0.0, keep)
    qa_ref[:, :d] = q
    qa_ref[:, d:] = keep.T.astype(BF16)
    qa = qa_ref[...]

    def block(n):
        kk = ka_ref[pl.ds(pl.multiple_of(n * blk, blk), blk), :]
        return lax.dot_general(kk, qa, _NT, preferred_element_type=F32) * scale, vt_ref[n]

    s, vt = block(i)
    s = s + bias_ref[0]
    key = lax.broadcasted_iota(I32, s.shape, 0)
    qry = lax.broadcasted_iota(I32, s.shape, 1)
    s = jnp.where(key <= qry, s, NEG_INF)
    m = jnp.max(s, axis=0, keepdims=True)
    p = jnp.exp(s - m)
    carry = (m, jnp.sum(p, axis=0, keepdims=True), _dot(vt, p.astype(BF16)))

    def adjacent(n, carry):
        s, vt = block(n)
        return _online_softmax_t(carry, s + bias_ref[1], vt)

    far_bias = rb_ref[REL_BUCKETS - 1, h]

    def far(n, carry):
        s, vt = block(n)
        return _online_softmax_t(carry, s + far_bias, vt)

    n_far = jnp.maximum(i - 1, 0)
    carry = lax.fori_loop(n_far, i, adjacent, carry)
    m, l, acc = lax.fori_loop(0, n_far, far, carry)
    o_ref[...] = (acc / l).T.astype(BF16)


def _moba_attn(qkv, et, bias, rel_bias, B, S):
    H, d, blk = MOBA_HEADS, MOBA_HEAD_DIM, MOBA_BLOCK
    T = B * S
    nb = S // blk
    n_sel = min(MOBA_TOPK, nb)
    return pl.pallas_call(
        functools.partial(_moba_attn_kernel, nb=nb, n_sel=n_sel, scale=d ** -0.5),
        grid=(B, H, nb),
        in_specs=[pl.BlockSpec(memory_space=pltpu.SMEM),
                  pl.BlockSpec((blk, d), lambda b, h, i: (b * nb + i, h)),
                  pl.BlockSpec((S, d), lambda b, h, i: (b, H + h)),
                  pl.BlockSpec((S, d), lambda b, h, i: (b, 2 * H + h)),
                  pl.BlockSpec((S, LANES), lambda b, h, i: (0, 0)),
                  pl.BlockSpec((None, 2, blk, blk), lambda b, h, i: (h, 0, 0, 0))],
        out_specs=pl.BlockSpec((blk, d), lambda b, h, i: (b * nb + i, h)),
        out_shape=SDS((T, H * d), BF16),
        scratch_shapes=[pltpu.VMEM((S, d + LANES), BF16),
                        pltpu.VMEM((LANES, d), F32),
                        pltpu.VMEM((blk, d + LANES), BF16),
                        pltpu.VMEM((nb, d, blk), BF16)],
        compiler_params=_params(3),
        name="moba_attn",
    )(rel_bias, qkv, qkv, qkv, et, bias)


def _out_proj_kernel(ya_ref, yb_ref, g_ref, x_ref, woa_ref, wob_ref, wout_ref, mg_ref, wr_ref, br_ref,
                     h1_ref, xn_ref, lg_ref, *, D):
    a = _dot(ya_ref[...], woa_ref[...])
    b = _dot(yb_ref[...], wob_ref[...])
    g = g_ref[...]
    merged = jax.nn.sigmoid(g[:, :D]) * a + jax.nn.sigmoid(g[:, D:]) * b
    h1 = x_ref[...] + _dot(merged.astype(BF16), wout_ref[...])
    h1_ref[...] = h1
    xn = _rms(h1, mg_ref[...])
    xn_ref[...] = xn
    lg_ref[...] = _dot(xn.astype(BF16), wr_ref[...]) + br_ref[...]


def _resident(a):
    return pl.BlockSpec(a.shape, lambda i: (0,) * a.ndim, pipeline_mode=pl.Buffered(1))


def _out_proj(ya, yb, gates, x2, woa, wob, wout, mgain, wr, br):
    T, D = x2.shape
    tm = min(OUT_TM, T)
    row = lambda a: pl.BlockSpec((tm, a.shape[1]), lambda i: (i, 0))
    return pl.pallas_call(
        functools.partial(_out_proj_kernel, D=D),
        grid=(T // tm,),
        in_specs=[row(ya), row(yb), row(gates), row(x2),
                  _resident(woa), _resident(wob), _resident(wout), _resident(mgain),
                  _resident(wr), _resident(br)],
        out_specs=[pl.BlockSpec((tm, D), lambda i: (i, 0)),
                   pl.BlockSpec((tm, D), lambda i: (i, 0)),
                   pl.BlockSpec((tm, LANES), lambda i: (i, 0))],
        out_shape=[SDS((T, D), F32), SDS((T, D), F32), SDS((T, LANES), F32)],
        compiler_params=_params(1),
        name="out_proj",
    )(ya, yb, gates, x2, woa, wob, wout, mgain, wr, br)


def _route_kernel(lg_ref, e_ref, w_ref, r_ref, cnt_ref, carry_ref, *, n_exp):
    i = pl.program_id(0)

    @pl.when(i == 0)
    def _():
        carry_ref[...] = jnp.zeros(carry_ref.shape, F32)

    lt = lg_ref[...].T
    tm = lt.shape[1]
    sub = lax.broadcasted_iota(I32, lt.shape, 0)
    neg = -jnp.inf
    cur = jnp.where(sub < n_exp, lt, neg)
    vals, idxs = [], []
    for _ in range(TOP_K):
        mx = jnp.max(cur, axis=0, keepdims=True)
        ix = jnp.min(jnp.where(cur == mx, sub, LANES), axis=0, keepdims=True)
        vals.append(mx)
        idxs.append(ix)
        cur = jnp.where(sub == ix, neg, cur)
    ex = [jnp.exp(v - vals[0]) for v in vals]
    den = ex[0]
    for e in ex[1:]:
        den = den + e
    onehot = jnp.zeros(lt.shape, F32)
    for ix in idxs:
        onehot = onehot + (sub == ix).astype(F32)
    r_i = lax.broadcasted_iota(I32, (tm, tm), 0)
    c_i = lax.broadcasted_iota(I32, (tm, tm), 1)
    tri = (r_i <= c_i).astype(BF16)
    incl = _dot(onehot.astype(BF16), tri)
    base = carry_ref[...]
    excl = incl - onehot + base[:, :1]
    e_ref[...] = jnp.zeros(e_ref.shape, I32)
    w_ref[...] = jnp.zeros(w_ref.shape, F32)
    r_ref[...] = jnp.zeros(r_ref.shape, I32)
    for k in range(TOP_K):
        e_ref[k:k + 1, :] = idxs[k]
        w_ref[k:k + 1, :] = ex[k] / den
        rk = jnp.sum(jnp.where(sub == idxs[k], excl, 0.0), axis=0, keepdims=True)
        r_ref[k:k + 1, :] = rk.astype(I32)
    total = base + jnp.sum(onehot, axis=1, keepdims=True)
    carry_ref[...] = total
    cnt_ref[...] = total


def _route(logits, n_exp):
    T = logits.shape[0]
    tm = min(ROUTE_TM, T)
    return pl.pallas_call(
        functools.partial(_route_kernel, n_exp=n_exp),
        grid=(T // tm,),
        in_specs=[pl.BlockSpec((tm, LANES), lambda i: (i, 0))],
        out_specs=[pl.BlockSpec((8, tm), lambda i: (0, i)),
                   pl.BlockSpec((8, tm), lambda i: (0, i)),
                   pl.BlockSpec((8, tm), lambda i: (0, i)),
                   pl.BlockSpec((LANES, LANES), lambda i: (0, 0))],
        out_shape=[SDS((8, T), I32), SDS((8, T), F32), SDS((8, T), I32), SDS((LANES, LANES), F32)],
        scratch_shapes=[pltpu.VMEM((LANES, LANES), F32)],
        compiler_params=_params(1),
        name="route",
    )(logits)


def _dest_kernel(e_ref, r_ref, base_ref, d_ref):
    base = base_ref[...][:, :1]
    sub = lax.broadcasted_iota(I32, (LANES, e_ref.shape[1]), 0)
    d_ref[...] = jnp.zeros(d_ref.shape, I32)
    for k in range(TOP_K):
        off = jnp.sum(jnp.where(sub == e_ref[k:k + 1, :], base, 0), axis=0, keepdims=True)
        d_ref[k:k + 1, :] = r_ref[k:k + 1, :] + off


def _dest(e_k, r_k, base):
    T = e_k.shape[1]
    tb = min(DEST_TB, T)
    blk = pl.BlockSpec((8, tb), lambda i: (0, i))
    return pl.pallas_call(
        _dest_kernel,
        grid=(T // tb,),
        in_specs=[blk, blk, pl.BlockSpec((LANES, LANES), lambda i: (0, 0))],
        out_specs=blk,
        out_shape=SDS((8, T), I32),
        compiler_params=_params(1),
        name="dest",
    )(e_k, r_k, base)


def _dispatch_kernel(zflag_ref, dest_ref, x_ref, xg_hbm, zbuf_ref, sem, *, tm, rows, n_tiles):
    i = pl.program_id(0)

    @pl.when(i == 0)
    def _():
        zbuf_ref[...] = jnp.zeros(zbuf_ref.shape, zbuf_ref.dtype)

        def zcopy(j):
            start = pl.multiple_of(j * rows, rows)
            return pltpu.make_async_copy(zbuf_ref, xg_hbm.at[pl.ds(start, rows)], sem)

        def zstart(j, c):
            @pl.when(zflag_ref[j] != 0)
            def _():
                zcopy(j).start()
            return c

        def zwait(j, c):
            @pl.when(zflag_ref[j] != 0)
            def _():
                zcopy(j).wait()
            return c

        lax.fori_loop(0, n_tiles, zstart, 0)
        lax.fori_loop(0, n_tiles, zwait, 0)

    def copy(t, k):
        return pltpu.make_async_copy(x_ref.at[pl.ds(t, 1)], xg_hbm.at[pl.ds(dest_ref[k, t], 1)], sem)

    def start(t, c):
        for k in range(TOP_K):
            copy(t, k).start()
        return c

    def wait(t, c):
        for k in range(TOP_K):
            copy(t, k).wait()
        return c

    lax.fori_loop(0, tm, start, 0, unroll=4)
    lax.fori_loop(0, tm, wait, 0, unroll=4)


def _dispatch(zflag, dest, xn, n_tiles):
    T, D = xn.shape
    tm = min(DISPATCH_TM, T)
    return pl.pallas_call(
        functools.partial(_dispatch_kernel, tm=tm, rows=EXPERT_ROWS, n_tiles=n_tiles),
        grid_spec=pltpu.PrefetchScalarGridSpec(
            num_scalar_prefetch=1,
            grid=(T // tm,),
            in_specs=[pl.BlockSpec((8, tm), lambda i, zf: (0, i), memory_space=pltpu.SMEM),
                      pl.BlockSpec((tm, D), lambda i, zf: (i, 0))],
            out_specs=pl.BlockSpec(memory_space=pl.ANY),
            scratch_shapes=[pltpu.VMEM((EXPERT_ROWS, D), F32), pltpu.SemaphoreType.DMA]),
        out_shape=SDS((n_tiles * EXPERT_ROWS, D), F32),
        compiler_params=_params(1),
        name="dispatch",
    )(zflag, dest, xn)


def _ffn_up_kernel(te_ref, nu_ref, x_ref, wg_ref, wu_ref, bg_ref, bu_ref, h_ref, wgb_ref, wub_ref):
    i = pl.program_id(1)
    used = i < nu_ref[0]
    new_expert = (i == 0) | (te_ref[i] != te_ref[jnp.maximum(i - 1, 0)])

    @pl.when(used & new_expert)
    def _():
        wgb_ref[...] = wg_ref[...].astype(BF16)
        wub_ref[...] = wu_ref[...].astype(BF16)

    @pl.when(used)
    def _():
        x = x_ref[...].astype(BF16)
        g = _dot(x, wgb_ref[...]) + bg_ref[...]
        u = _dot(x, wub_ref[...]) + bu_ref[...]
        g = jnp.minimum(g, SWIGLU_LIMIT)
        u = jnp.clip(u, -SWIGLU_LIMIT, SWIGLU_LIMIT)
        h_ref[...] = (g * jax.nn.sigmoid(SWIGLU_ALPHA * g) * (u + 1.0)).astype(BF16)

    @pl.when(jnp.logical_not(used))
    def _():
        h_ref[...] = jnp.zeros(h_ref.shape, h_ref.dtype)


def _ffn_up(tile_expert, n_used, xg, wg, wu, bg, bu):
    P, D = xg.shape
    Dx = wg.shape[2]
    tr = EXPERT_ROWS
    tn = min(EXPERT_TN, Dx)
    n_tiles = P // tr
    xmap = lambda j, i, te, nu: (jnp.minimum(i, nu[0] - 1), 0)
    wmap = lambda j, i, te, nu: (te[i], 0, j)
    return pl.pallas_call(
        _ffn_up_kernel,
        grid_spec=pltpu.PrefetchScalarGridSpec(
            num_scalar_prefetch=2,
            grid=(Dx // tn, n_tiles),
            in_specs=[pl.BlockSpec((tr, D), xmap),
                      pl.BlockSpec((None, D, tn), wmap),
                      pl.BlockSpec((None, D, tn), wmap),
                      pl.BlockSpec((None, 1, tn), wmap),
                      pl.BlockSpec((None, 1, tn), wmap)],
            out_specs=pl.BlockSpec((tr, tn), lambda j, i, te, nu: (i, j)),
            scratch_shapes=[pltpu.VMEM((D, tn), BF16), pltpu.VMEM((D, tn), BF16)]),
        out_shape=SDS((P, Dx), BF16),
        compiler_params=_params(2),
        name="ffn_up",
    )(tile_expert, n_used, xg, wg, wu, bg, bu)


def _ffn_down_kernel(te_ref, nu_ref, h_ref, wd_ref, bd_ref, y_ref, wdb_ref):
    i = pl.program_id(1)
    used = i < nu_ref[0]
    new_expert = (i == 0) | (te_ref[i] != te_ref[jnp.maximum(i - 1, 0)])

    @pl.when(used & new_expert)
    def _():
        wdb_ref[...] = wd_ref[...].astype(BF16)

    @pl.when(used)
    def _():
        y_ref[...] = _dot(h_ref[...], wdb_ref[...]) + bd_ref[...]

    @pl.when(jnp.logical_not(used))
    def _():
        y_ref[...] = jnp.zeros(y_ref.shape, y_ref.dtype)


def _ffn_down(tile_expert, n_used, hid, wd, bd):
    P, Dx = hid.shape
    D = wd.shape[2]
    tr = EXPERT_ROWS
    tn = min(EXPERT_TN, D)
    n_tiles = P // tr
    hmap = lambda j, i, te, nu: (jnp.minimum(i, nu[0] - 1), 0)
    wmap = lambda j, i, te, nu: (te[i], 0, j)
    return pl.pallas_call(
        _ffn_down_kernel,
        grid_spec=pltpu.PrefetchScalarGridSpec(
            num_scalar_prefetch=2,
            grid=(D // tn, n_tiles),
            in_specs=[pl.BlockSpec((tr, Dx), hmap),
                      pl.BlockSpec((None, Dx, tn), wmap),
                      pl.BlockSpec((None, 1, tn), wmap)],
            out_specs=pl.BlockSpec((tr, tn), lambda j, i, te, nu: (i, j)),
            scratch_shapes=[pltpu.VMEM((Dx, tn), BF16)]),
        out_shape=SDS((P, D), F32),
        compiler_params=_params(2),
        name="ffn_down",
    )(tile_expert, n_used, hid, wd, bd)


def _combine_kernel(dest_ref, w_ref, h1_ref, p_ref, wple_ref, wpg_ref, pg_ref, fg_ref, y_hbm,
                    o_ref, ybuf_ref, sem, *, tm, final):
    def copy(t, k):
        return pltpu.make_async_copy(y_hbm.at[pl.ds(dest_ref[k, t], 1)],
                                     ybuf_ref.at[k, pl.ds(t, 1)], sem)

    def start(t, c):
        for k in range(TOP_K):
            copy(t, k).start()
        return c

    def wait(t, c):
        for k in range(TOP_K):
            copy(t, k).wait()
        return c

    lax.fori_loop(0, tm, start, 0, unroll=4)
    pw = _dot(p_ref[...].astype(BF16), wple_ref[...])
    lax.fori_loop(0, tm, wait, 0, unroll=4)
    w = w_ref[...]
    moe = w[:, 0:1] * ybuf_ref[0]
    for k in range(1, TOP_K):
        moe = moe + w[:, k:k + 1] * ybuf_ref[k]
    h2 = h1_ref[...] + moe
    xn = _rms(h2, pg_ref[...]).astype(BF16)
    gate = jax.nn.sigmoid(_dot(xn, wpg_ref[...]))
    h3 = h2 + pw * gate
    o_ref[...] = _rms(h3, fg_ref[...]) if final else h3


def _combine(dest, wts, h1, p2, wple, wpg, pgain, fgain, yg, final):
    T, D = h1.shape
    tm = min(COMBINE_TM, T)
    row = lambda a: pl.BlockSpec((tm, a.shape[1]), lambda i: (i, 0))
    return pl.pallas_call(
        functools.partial(_combine_kernel, tm=tm, final=final),
        grid=(T // tm,),
        in_specs=[pl.BlockSpec((8, tm), lambda i: (0, i), memory_space=pltpu.SMEM),
                  row(wts), row(h1), row(p2),
                  _resident(wple), _resident(wpg), _resident(pgain), _resident(fgain),
                  pl.BlockSpec(memory_space=pl.ANY)],
        out_specs=pl.BlockSpec((tm, D), lambda i: (i, 0)),
        out_shape=SDS((T, D), F32),
        scratch_shapes=[pltpu.VMEM((TOP_K, tm, D), F32), pltpu.SemaphoreType.DMA],
        compiler_params=_params(1),
        name="combine",
    )(dest, wts, h1, p2, wple, wpg, pgain, fgain, yg)


def kernel(x, p, attn_norm, w_in, q_lat_norm, kv_lat_norm, w_uq, w_ukv, w_o_mla, w_o_moba, w_out,
           rel_bias, moe_norm, w_router, b_router, w_gate, b_gate, w_up, b_up, w_down, b_down,
           ple_norm, w_ple_gate, w_ple, final_norm):
    B, S, D = x.shape
    T = B * S
    n_layers = w_in.shape[0]
    E = w_router.shape[-1]
    H = MLA_HEADS
    mw = MOBA_HEADS * MOBA_HEAD_DIM
    assert S % MOBA_BLOCK == 0 and E <= LANES
    o_kr = MLA_Q_LORA + MLA_KV_LORA
    o_q = o_kr + MLA_ROPE
    o_g = o_q + 3 * mw

    inv = 1.0 / (ROPE_THETA ** (jnp.arange(0, MLA_ROPE, 2, dtype=F32) / MLA_ROPE))
    ang = jnp.arange(S, dtype=F32)[:, None] * inv[None, :]
    cos, sin = jnp.cos(ang), jnp.sin(ang)
    zpad = jnp.zeros((S, LANES - MLA_ROPE), F32)
    cosw = jnp.concatenate([cos, cos, zpad], axis=1)
    sinw = jnp.concatenate([-sin, sin, zpad], axis=1)
    r = jnp.arange(MOBA_BLOCK)
    d0 = r[None, :] - r[:, None]
    bidx = jnp.stack([_t5_bucket(d0), _t5_bucket(d0 + MOBA_BLOCK)]).astype(I32)
    et = (jnp.arange(S)[:, None] // MOBA_BLOCK == jnp.arange(LANES)[None, :]).astype(BF16)
    bias = _moba_bias(rel_bias, bidx)

    h = x.reshape(T, D)
    for li in range(n_layers):
        w = w_in[li]
        w_main = jnp.concatenate([w[:, :o_kr], w[:, o_q:]], axis=1).astype(BF16)
        w_kr = jnp.pad(w[:, o_kr:o_q], ((0, 0), (0, LANES - MLA_ROPE))).astype(BF16)
        wq = w_uq[li].reshape(MLA_Q_LORA, H, MLA_NOPE + MLA_ROPE)
        wq = jnp.pad(wq, ((0, 0), (0, 0), (0, MLA_SLOT - MLA_NOPE - MLA_ROPE)))
        wq = wq.reshape(MLA_Q_LORA, H * MLA_SLOT).astype(BF16)
        wkv = w_ukv[li].reshape(MLA_KV_LORA, H, MLA_NOPE + MLA_V)
        wk = wkv[:, :, :MLA_NOPE].reshape(MLA_KV_LORA, H * MLA_NOPE).astype(BF16)
        wv = wkv[:, :, MLA_NOPE:].reshape(MLA_KV_LORA, H * MLA_V).astype(BF16)
        wr = jnp.pad(w_router[li], ((0, 0), (0, LANES - E))).astype(BF16)
        br = jnp.pad(b_router[li], (0, LANES - E), constant_values=NEG_INF)[None, :]

        lat, qkv, gates, kr = _in_proj(h, attn_norm[li][None, :], w_main, w_kr, o_kr, 3 * mw)
        q_a, k_a, v_a = _mla_proj(lat, kr, cosw, sinw, q_lat_norm[li][None, :], kv_lat_norm[li][None, :],
                                  wq, wk, wv, S)
        y_a = _mla_attn(q_a, k_a, v_a, B, S)
        y_b = _moba_attn(qkv, et, bias, rel_bias, B, S)
        h1, xn, logits = _out_proj(y_a, y_b, gates, h, w_o_mla[li].astype(BF16), w_o_moba[li].astype(BF16),
                                   w_out[li].astype(BF16), moe_norm[li][None, :], wr, br)

        e_k, w_k, r_k, cnt = _route(logits, E)
        counts = cnt[:E, 0].astype(I32)
        tiles = (counts + EXPERT_ROWS - 1) // EXPERT_ROWS
        tile_end = jnp.cumsum(tiles)
        tile_start = tile_end - tiles
        n_tiles = (T * TOP_K) // EXPERT_ROWS + E
        n_used = tile_end[-1]
        tile_ids = jnp.arange(n_tiles)
        capped = jnp.minimum(tile_ids, n_used - 1)
        tile_expert = jnp.minimum(jnp.sum(tile_end[None, :] <= capped[:, None], axis=1), E - 1).astype(I32)
        base = jnp.pad(tile_start * EXPERT_ROWS, (0, LANES - E)).astype(I32)
        dest = _dest(e_k, r_k, jnp.broadcast_to(base[:, None], (LANES, LANES)))
        is_last = jnp.any((tile_ids[:, None] == tile_end[None, :] - 1) & (tiles[None, :] > 0), axis=1)
        zflag = (is_last | (tile_ids >= n_used)).astype(I32)
        nu = n_used.astype(I32)[None]

        xg = _dispatch(zflag, dest, xn, n_tiles)
        hid = _ffn_up(tile_expert, nu, xg, w_gate[li], w_up[li], b_gate[li][:, None, :], b_up[li][:, None, :])
        yg = _ffn_down(tile_expert, nu, hid, w_down[li], b_down[li][:, None, :])
        h = _combine(dest, w_k[:TOP_K].T, h1, p[li].reshape(T, -1), w_ple[li].astype(BF16),
                     w_ple_gate[li].astype(BF16), ple_norm[li][None, :], final_norm[None, :], yg,
                     final=li == n_layers - 1)
    return h.reshape(B, S, D)
```

```python
import functools
import math

import jax
import jax.numpy as jnp
from jax import lax
from jax.experimental import pallas as pl
from jax.experimental.pallas import tpu as pltpu

F32 = jnp.float32
BF16 = jnp.bfloat16
I32 = jnp.int32
SDS = jax.ShapeDtypeStruct

EPS = 1e-6
NEG_INF = -1e30
MLA_HEADS = 8
MLA_NOPE = 128
MLA_ROPE = 64
MLA_V = 128
MLA_Q_LORA = 512
MLA_KV_LORA = 512
ROPE_THETA = 10000.0
MOBA_HEADS = 8
MOBA_HEAD_DIM = 128
MOBA_BLOCK = 256
MOBA_TOPK = 3
REL_BUCKETS = 32
REL_MAX_DIST = 128
TOP_K = 4
SWIGLU_LIMIT = 7.0
SWIGLU_ALPHA = 1.702

LANES = 128
MLA_SLOT = 2 * LANES
VMEM_LIMIT = 56 * 2**20

IN_TM, IN_TN = 512, 1024
MLAP_TM = 512
MLA_TQ = 512
MLA_TK = 512
MLA_HEADS_PER_STEP = 2
MOBA_HEADS_PER_STEP = 4
OUT_TM = 256
ROUTE_TM = 512
EXPERT_ROWS = 1024
EXPERT_SUB = 256
EXPERT_TN_UP = 512
EXPERT_TN_DOWN = 1024
ZERO_ROWS = 512
DISPATCH_TM = 512
DEST_TB = 2048
COMBINE_TM = 256

_NT = (((1,), (1,)), ((), ()))


def _params(n_axes):
    return pltpu.CompilerParams(dimension_semantics=("arbitrary",) * n_axes,
                                vmem_limit_bytes=VMEM_LIMIT)


def _rms(x, g):
    return x * lax.rsqrt(jnp.mean(x * x, axis=-1, keepdims=True) + EPS) * g


def _dot(a, b):
    return jnp.dot(a, b, preferred_element_type=F32)


def _in_proj_kernel(x_ref, g_ref, w_ref, wkr_ref, lat_ref, qkv_ref, gates_ref, kr_ref, xn_ref,
                    *, n_lat, n_qkv):
    j = pl.program_id(1)

    @pl.when(j == 0)
    def _():
        xn = _rms(x_ref[...], g_ref[...]).astype(BF16)
        xn_ref[...] = xn
        kr_ref[...] = _dot(xn, wkr_ref[...])

    acc = _dot(xn_ref[...], w_ref[...])

    @pl.when(j < n_lat)
    def _():
        lat_ref[...] = acc

    @pl.when((j >= n_lat) & (j < n_lat + n_qkv))
    def _():
        qkv_ref[...] = acc.astype(BF16)

    @pl.when(j >= n_lat + n_qkv)
    def _():
        gates_ref[...] = acc


def _in_proj(x2, gain, w_main, w_kr, n_lat_cols, n_qkv_cols):
    T, D = x2.shape
    n_g_cols = w_main.shape[1] - n_lat_cols - n_qkv_cols
    tm = min(IN_TM, T)
    tn = math.gcd(math.gcd(IN_TN, n_lat_cols), math.gcd(n_qkv_cols, n_g_cols))
    n_lat, n_qkv, n_g = n_lat_cols // tn, n_qkv_cols // tn, n_g_cols // tn
    return pl.pallas_call(
        functools.partial(_in_proj_kernel, n_lat=n_lat, n_qkv=n_qkv),
        grid=(T // tm, n_lat + n_qkv + n_g),
        in_specs=[pl.BlockSpec((tm, D), lambda i, j: (i, 0)),
                  pl.BlockSpec((1, D), lambda i, j: (0, 0)),
                  pl.BlockSpec((D, tn), lambda i, j: (0, j)),
                  pl.BlockSpec((D, LANES), lambda i, j: (0, 0))],
        out_specs=[pl.BlockSpec((tm, tn), lambda i, j: (i, jnp.minimum(j, n_lat - 1))),
                   pl.BlockSpec((tm, tn), lambda i, j: (i, jnp.clip(j - n_lat, 0, n_qkv - 1))),
                   pl.BlockSpec((tm, tn), lambda i, j: (i, jnp.maximum(j - n_lat - n_qkv, 0))),
                   pl.BlockSpec((tm, LANES), lambda i, j: (i, 0))],
        out_shape=[SDS((T, n_lat_cols), F32), SDS((T, n_qkv_cols), BF16),
                   SDS((T, n_g_cols), F32), SDS((T, LANES), F32)],
        scratch_shapes=[pltpu.VMEM((tm, D), BF16)],
        compiler_params=_params(2),
        name="in_proj",
    )(x2, gain, w_main, w_kr)


def _mla_proj_kernel(lat_ref, kr_ref, cos_ref, sin_ref, qn_ref, kvn_ref, wq_ref, wk_ref, wv_ref,
                     q_ref, k_ref, v_ref):
    lat = lat_ref[...]
    qn = _rms(lat[:, :MLA_Q_LORA], qn_ref[...]).astype(BF16)
    kvn = _rms(lat[:, MLA_Q_LORA:], kvn_ref[...]).astype(BF16)
    q = _dot(qn, wq_ref[...])
    kn = _dot(kvn, wk_ref[...])
    v_ref[...] = _dot(kvn, wv_ref[...]).astype(BF16)
    c = cos_ref[...]
    s = sin_ref[...]
    half = MLA_ROPE // 2
    lane = lax.broadcasted_iota(I32, c.shape, 1)

    def rope(xr):
        swapped = jnp.where(lane < half, pltpu.roll(xr, LANES - half, 1), pltpu.roll(xr, half, 1))
        return xr * c + swapped * s

    kr = rope(kr_ref[...]).astype(BF16)
    for h in range(MLA_HEADS):
        lo = h * MLA_SLOT
        q_ref[:, lo:lo + LANES] = q[:, lo:lo + LANES].astype(BF16)
        q_ref[:, lo + LANES:lo + MLA_SLOT] = rope(q[:, lo + LANES:lo + MLA_SLOT]).astype(BF16)
        k_ref[:, lo:lo + LANES] = kn[:, h * MLA_NOPE:(h + 1) * MLA_NOPE].astype(BF16)
        k_ref[:, lo + LANES:lo + MLA_SLOT] = kr


def _mla_proj(lat, kr, cosw, sinw, qnorm, kvnorm, wq, wk, wv, S):
    T = lat.shape[0]
    tm = min(MLAP_TM, S)
    ns = S // tm
    H = MLA_HEADS
    full = lambda a: pl.BlockSpec(a.shape, lambda i: (0,) * a.ndim)
    return pl.pallas_call(
        _mla_proj_kernel,
        grid=(T // tm,),
        in_specs=[pl.BlockSpec((tm, lat.shape[1]), lambda i: (i, 0)),
                  pl.BlockSpec((tm, LANES), lambda i: (i, 0)),
                  pl.BlockSpec((tm, LANES), lambda i: (i % ns, 0)),
                  pl.BlockSpec((tm, LANES), lambda i: (i % ns, 0)),
                  full(qnorm), full(kvnorm), full(wq), full(wk), full(wv)],
        out_specs=[pl.BlockSpec((tm, H * MLA_SLOT), lambda i: (i, 0)),
                   pl.BlockSpec((tm, H * MLA_SLOT), lambda i: (i, 0)),
                   pl.BlockSpec((tm, H * MLA_V), lambda i: (i, 0))],
        out_shape=[SDS((T, H * MLA_SLOT), BF16), SDS((T, H * MLA_SLOT), BF16),
                   SDS((T, H * MLA_V), BF16)],
        compiler_params=_params(1),
        name="mla_proj",
    )(lat, kr, cosw, sinw, qnorm, kvnorm, wq, wk, wv)


def _transpose_chunks(src_ref, dst_ref):
    n, _, tk = dst_ref.shape
    for c in range(n):
        dst_ref[c] = src_ref[c * tk:(c + 1) * tk, :].astype(F32).T.astype(dst_ref.dtype)


def _online_softmax_t(carry, s, vt):
    return _online_softmax_heads((carry,), (s,), (vt,))[0]


def _online_softmax_heads(carries, ss, vts):
    stats = []
    for (m, l, acc), s in zip(carries, ss):
        m_new = jnp.maximum(m, jnp.max(s, axis=0, keepdims=True))
        alpha = jnp.exp(m - m_new)
        p = jnp.exp(s - m_new)
        stats.append((m_new, alpha * l + jnp.sum(p, axis=0, keepdims=True), alpha, p.astype(BF16)))
    return tuple((m_new, l, alpha * acc + _dot(vt, p))
                 for (m_new, l, alpha, p), (_, _, acc), vt in zip(stats, carries, vts))


def _mla_attn_kernel(q_ref, k_ref, v_ref, o_ref, vt_ref, *, tq, tk, scale, heads):
    qi = pl.program_id(2)
    slot, dv = MLA_SLOT, MLA_V

    @pl.when(qi == 0)
    def _():
        for g in range(heads):
            _transpose_chunks(v_ref.at[:, g * dv:(g + 1) * dv], vt_ref.at[g])

    qs = [q_ref[:, g * slot:(g + 1) * slot] for g in range(heads)]
    per_q = tq // tk

    def chunk(g, c):
        k = k_ref[pl.ds(pl.multiple_of(c * tk, tk), tk), g * slot:(g + 1) * slot]
        return lax.dot_general(k, qs[g], _NT, preferred_element_type=F32) * scale, vt_ref[g, c]

    carry = [None] * heads
    for c in range(per_q):
        for g in range(heads):
            s, vt = chunk(g, qi * per_q + c)
            key = lax.broadcasted_iota(I32, s.shape, 0) + c * tk
            qry = lax.broadcasted_iota(I32, s.shape, 1)
            s = jnp.where(key <= qry, s, NEG_INF)
            if carry[g] is None:
                m = jnp.max(s, axis=0, keepdims=True)
                p = jnp.exp(s - m)
                carry[g] = (m, jnp.sum(p, axis=0, keepdims=True), _dot(vt, p.astype(BF16)))
            else:
                carry[g] = _online_softmax_t(carry[g], s, vt)

    def body(c, carry):
        ss, vts = zip(*[chunk(g, c) for g in range(heads)])
        return _online_softmax_heads(carry, ss, vts)

    carry = lax.fori_loop(0, qi * per_q, body, tuple(carry))
    for g in range(heads):
        m, l, acc = carry[g]
        o_ref[:, g * dv:(g + 1) * dv] = (acc / l).T.astype(BF16)


def _mla_attn(q, k, v, B, S):
    H = MLA_HEADS
    G = MLA_HEADS_PER_STEP
    T = B * S
    tq = min(MLA_TQ, S)
    tk = min(MLA_TK, tq)
    nq = S // tq
    scale = (MLA_NOPE + MLA_ROPE) ** -0.5
    return pl.pallas_call(
        functools.partial(_mla_attn_kernel, tq=tq, tk=tk, scale=scale, heads=G),
        grid=(B, H // G, nq),
        in_specs=[pl.BlockSpec((tq, G * MLA_SLOT), lambda b, h, i: (b * nq + i, h)),
                  pl.BlockSpec((S, G * MLA_SLOT), lambda b, h, i: (b, h)),
                  pl.BlockSpec((S, G * MLA_V), lambda b, h, i: (b, h))],
        out_specs=pl.BlockSpec((tq, G * MLA_V), lambda b, h, i: (b * nq + i, h)),
        out_shape=SDS((T, H * MLA_V), BF16),
        scratch_shapes=[pltpu.VMEM((G, S // tk, MLA_V, tk), BF16)],
        compiler_params=_params(3),
        name="mla_attn",
    )(q, k, v)


def _t5_bucket(dist):
    n = jnp.maximum(dist, 0)
    max_exact = REL_BUCKETS // 2
    large = max_exact + (jnp.log(jnp.maximum(n, 1).astype(F32) / max_exact)
                         / math.log(REL_MAX_DIST / max_exact)
                         * (REL_BUCKETS - max_exact)).astype(I32)
    large = jnp.minimum(large, REL_BUCKETS - 1)
    return jnp.where(n < max_exact, n, large)


def _moba_bias_kernel(rb_ref, bidx_ref, o_ref):
    h = pl.program_id(0)
    for t in range(2):
        bi = bidx_ref[t]
        val = jnp.zeros(bi.shape, F32)
        for b in range(REL_BUCKETS):
            val = jnp.where(bi == b, rb_ref[b, h], val)
        o_ref[t] = val


def _moba_bias(rel_bias, bidx):
    H = rel_bias.shape[1]
    blk = MOBA_BLOCK
    return pl.pallas_call(
        _moba_bias_kernel,
        grid=(H,),
        in_specs=[pl.BlockSpec(memory_space=pltpu.SMEM),
                  pl.BlockSpec((2, blk, blk), lambda h: (0, 0, 0))],
        out_specs=pl.BlockSpec((None, 2, blk, blk), lambda h: (h, 0, 0, 0)),
        out_shape=SDS((H, 2, blk, blk), F32),
        compiler_params=_params(1),
        name="moba_bias",
    )(rel_bias, bidx)


def _moba_attn_kernel(rb_ref, q_ref, k_ref, v_ref, et_ref, bias_ref, o_ref, ka_ref, km_ref, qa_ref, vt_ref,
                      *, nb, n_sel, scale, heads):
    hg = pl.program_id(1)
    i = pl.program_id(2)
    blk = MOBA_BLOCK
    d = MOBA_HEAD_DIM

    @pl.when(i == 0)
    def _():
        for g in range(heads):
            cols = slice(g * d, (g + 1) * d)
            ka_ref[g, :, :d] = k_ref[:, cols]
            ka_ref[g, :, d:] = et_ref[...]
            _transpose_chunks(v_ref.at[:, cols], vt_ref.at[g])
            km_ref[g] = jnp.zeros(km_ref.shape[1:], F32)
            for n in range(nb):
                kb = k_ref[n * blk:(n + 1) * blk, cols].astype(F32)
                km_ref[g, n:n + 1, :] = jnp.sum(kb, axis=0, keepdims=True) * (1.0 / blk)

    qas = []
    for g in range(heads):
        q = q_ref[:, g * d:(g + 1) * d]
        gate = lax.dot_general(km_ref[g].astype(BF16), q, _NT, preferred_element_type=F32)
        sub = lax.broadcasted_iota(I32, gate.shape, 0)
        gt = jnp.where(sub < i, gate, NEG_INF)
        keep = jnp.full(gate.shape, NEG_INF, F32)
        for _ in range(n_sel):
            mx = jnp.max(gt, axis=0, keepdims=True)
            first = jnp.min(jnp.where(gt == mx, sub, LANES), axis=0, keepdims=True)
            pick = sub == first
            keep = jnp.where(pick & (sub < i), 0.0, keep)
            gt = jnp.where(pick, -3.0e38, gt)
        keep = jnp.where(sub == i, 0.0, keep)
        qa_ref[g, :, :d] = q
        qa_ref[g, :, d:] = keep.T.astype(BF16)
        qas.append(qa_ref[g])

    def block(g, n):
        kk = ka_ref[g, pl.ds(pl.multiple_of(n * blk, blk), blk), :]
        return lax.dot_general(kk, qas[g], _NT, preferred_element_type=F32) * scale, vt_ref[g, n]

    carry = []
    for g in range(heads):
        s, vt = block(g, i)
        s = s + bias_ref[g, 0]
        key = lax.broadcasted_iota(I32, s.shape, 0)
        qry = lax.broadcasted_iota(I32, s.shape, 1)
        s = jnp.where(key <= qry, s, NEG_INF)
        m = jnp.max(s, axis=0, keepdims=True)
        p = jnp.exp(s - m)
        carry.append((m, jnp.sum(p, axis=0, keepdims=True), _dot(vt, p.astype(BF16))))

    def adjacent(n, carry):
        ss, vts = zip(*[block(g, n) for g in range(heads)])
        ss = [s + bias_ref[g, 1] for g, s in enumerate(ss)]
        return _online_softmax_heads(carry, ss, vts)

    far_bias = [rb_ref[REL_BUCKETS - 1, hg * heads + g] for g in range(heads)]

    def far(n, carry):
        ss, vts = zip(*[block(g, n) for g in range(heads)])
        ss = [s + far_bias[g] for g, s in enumerate(ss)]
        return _online_softmax_heads(carry, ss, vts)

    n_far = jnp.maximum(i - 1, 0)
    carry = lax.fori_loop(n_far, i, adjacent, tuple(carry))
    carry = lax.fori_loop(0, n_far, far, carry)
    for g in range(heads):
        m, l, acc = carry[g]
        o_ref[:, g * d:(g + 1) * d] = (acc / l).T.astype(BF16)


def _moba_attn(qkv, et, bias, rel_bias, B, S):
    H, d, blk = MOBA_HEADS, MOBA_HEAD_DIM, MOBA_BLOCK
    T = B * S
    nb = S // blk
    n_sel = min(MOBA_TOPK, nb)
    G = MOBA_HEADS_PER_STEP
    ng = H // G
    return pl.pallas_call(
        functools.partial(_moba_attn_kernel, nb=nb, n_sel=n_sel, scale=d ** -0.5, heads=G),
        grid=(B, ng, nb),
        in_specs=[pl.BlockSpec(memory_space=pltpu.SMEM),
                  pl.BlockSpec((blk, G * d), lambda b, h, i: (b * nb + i, h)),
                  pl.BlockSpec((S, G * d), lambda b, h, i: (b, ng + h)),
                  pl.BlockSpec((S, G * d), lambda b, h, i: (b, 2 * ng + h)),
                  pl.BlockSpec((S, LANES), lambda b, h, i: (0, 0)),
                  pl.BlockSpec((G, 2, blk, blk), lambda b, h, i: (h, 0, 0, 0))],
        out_specs=pl.BlockSpec((blk, G * d), lambda b, h, i: (b * nb + i, h)),
        out_shape=SDS((T, H * d), BF16),
        scratch_shapes=[pltpu.VMEM((G, S, d + LANES), BF16),
                        pltpu.VMEM((G, LANES, d), F32),
                        pltpu.VMEM((G, blk, d + LANES), BF16),
                        pltpu.VMEM((G, nb, d, blk), BF16)],
        compiler_params=_params(3),
        name="moba_attn",
    )(rel_bias, qkv, qkv, qkv, et, bias)


def _out_proj_kernel(ya_ref, yb_ref, g_ref, x_ref, woa_ref, wob_ref, wout_ref, mg_ref, wr_ref, br_ref,
                     h1_ref, xn_ref, lg_ref, *, D):
    a = _dot(ya_ref[...], woa_ref[...])
    b = _dot(yb_ref[...], wob_ref[...])
    g = g_ref[...]
    merged = jax.nn.sigmoid(g[:, :D]) * a + jax.nn.sigmoid(g[:, D:]) * b
    h1 = x_ref[...] + _dot(merged.astype(BF16), wout_ref[...])
    h1_ref[...] = h1
    xn = _rms(h1, mg_ref[...])
    xn_ref[...] = xn
    lg_ref[...] = _dot(xn.astype(BF16), wr_ref[...]) + br_ref[...]


def _resident(a):
    return pl.BlockSpec(a.shape, lambda i: (0,) * a.ndim, pipeline_mode=pl.Buffered(1))


def _out_proj(ya, yb, gates, x2, woa, wob, wout, mgain, wr, br):
    T, D = x2.shape
    tm = min(OUT_TM, T)
    row = lambda a: pl.BlockSpec((tm, a.shape[1]), lambda i: (i, 0))
    return pl.pallas_call(
        functools.partial(_out_proj_kernel, D=D),
        grid=(T // tm,),
        in_specs=[row(ya), row(yb), row(gates), row(x2),
                  _resident(woa), _resident(wob), _resident(wout), _resident(mgain),
                  _resident(wr), _resident(br)],
        out_specs=[pl.BlockSpec((tm, D), lambda i: (i, 0)),
                   pl.BlockSpec((tm, D), lambda i: (i, 0)),
                   pl.BlockSpec((tm, LANES), lambda i: (i, 0))],
        out_shape=[SDS((T, D), F32), SDS((T, D), F32), SDS((T, LANES), F32)],
        compiler_params=_params(1),
        name="out_proj",
    )(ya, yb, gates, x2, woa, wob, wout, mgain, wr, br)


def _route_kernel(lg_ref, e_ref, w_ref, r_ref, cnt_ref, carry_ref, *, n_exp):
    i = pl.program_id(0)

    @pl.when(i == 0)
    def _():
        carry_ref[...] = jnp.zeros(carry_ref.shape, F32)

    lt = lg_ref[...].T
    tm = lt.shape[1]
    sub = lax.broadcasted_iota(I32, lt.shape, 0)
    neg = -jnp.inf
    cur = jnp.where(sub < n_exp, lt, neg)
    vals, idxs = [], []
    for _ in range(TOP_K):
        mx = jnp.max(cur, axis=0, keepdims=True)
        ix = jnp.min(jnp.where(cur == mx, sub, LANES), axis=0, keepdims=True)
        vals.append(mx)
        idxs.append(ix)
        cur = jnp.where(sub == ix, neg, cur)
    ex = [jnp.exp(v - vals[0]) for v in vals]
    den = ex[0]
    for e in ex[1:]:
        den = den + e
    onehot = jnp.zeros(lt.shape, F32)
    for ix in idxs:
        onehot = onehot + (sub == ix).astype(F32)
    r_i = lax.broadcasted_iota(I32, (tm, tm), 0)
    c_i = lax.broadcasted_iota(I32, (tm, tm), 1)
    tri = (r_i <= c_i).astype(BF16)
    incl = _dot(onehot.astype(BF16), tri)
    base = carry_ref[...]
    excl = incl - onehot + base[:, :1]
    e_ref[...] = jnp.zeros(e_ref.shape, I32)
    w_ref[...] = jnp.zeros(w_ref.shape, F32)
    r_ref[...] = jnp.zeros(r_ref.shape, I32)
    for k in range(TOP_K):
        e_ref[k:k + 1, :] = idxs[k]
        w_ref[k:k + 1, :] = ex[k] / den
        rk = jnp.sum(jnp.where(sub == idxs[k], excl, 0.0), axis=0, keepdims=True)
        r_ref[k:k + 1, :] = rk.astype(I32)
    total = base + jnp.sum(onehot, axis=1, keepdims=True)
    carry_ref[...] = total
    cnt_ref[...] = total


def _route(logits, n_exp):
    T = logits.shape[0]
    tm = min(ROUTE_TM, T)
    return pl.pallas_call(
        functools.partial(_route_kernel, n_exp=n_exp),
        grid=(T // tm,),
        in_specs=[pl.BlockSpec((tm, LANES), lambda i: (i, 0))],
        out_specs=[pl.BlockSpec((8, tm), lambda i: (0, i)),
                   pl.BlockSpec((8, tm), lambda i: (0, i)),
                   pl.BlockSpec((8, tm), lambda i: (0, i)),
                   pl.BlockSpec((LANES, LANES), lambda i: (0, 0))],
        out_shape=[SDS((8, T), I32), SDS((8, T), F32), SDS((8, T), I32), SDS((LANES, LANES), F32)],
        scratch_shapes=[pltpu.VMEM((LANES, LANES), F32)],
        compiler_params=_params(1),
        name="route",
    )(logits)


def _dest_kernel(e_ref, r_ref, base_ref, d_ref):
    base = base_ref[...][:, :1]
    sub = lax.broadcasted_iota(I32, (LANES, e_ref.shape[1]), 0)
    d_ref[...] = jnp.zeros(d_ref.shape, I32)
    for k in range(TOP_K):
        off = jnp.sum(jnp.where(sub == e_ref[k:k + 1, :], base, 0), axis=0, keepdims=True)
        d_ref[k:k + 1, :] = r_ref[k:k + 1, :] + off


def _dest(e_k, r_k, base):
    T = e_k.shape[1]
    tb = min(DEST_TB, T)
    blk = pl.BlockSpec((8, tb), lambda i: (0, i))
    return pl.pallas_call(
        _dest_kernel,
        grid=(T // tb,),
        in_specs=[blk, blk, pl.BlockSpec((LANES, LANES), lambda i: (0, 0))],
        out_specs=blk,
        out_shape=SDS((8, T), I32),
        compiler_params=_params(1),
        name="dest",
    )(e_k, r_k, base)


def _dispatch_kernel(zflag_ref, dest_ref, x_ref, xg_hbm, zbuf_ref, sem, *, tm, rows, n_tiles):
    i = pl.program_id(0)

    @pl.when(i == 0)
    def _():
        zbuf_ref[...] = jnp.zeros(zbuf_ref.shape, zbuf_ref.dtype)

        def zcopy(j):
            start = pl.multiple_of(j * rows, rows)
            return pltpu.make_async_copy(zbuf_ref, xg_hbm.at[pl.ds(start, rows)], sem)

        def zstart(j, c):
            @pl.when(zflag_ref[j] != 0)
            def _():
                zcopy(j).start()
            return c

        def zwait(j, c):
            @pl.when(zflag_ref[j] != 0)
            def _():
                zcopy(j).wait()
            return c

        lax.fori_loop(0, n_tiles, zstart, 0)
        lax.fori_loop(0, n_tiles, zwait, 0)

    def copy(t, k):
        return pltpu.make_async_copy(x_ref.at[pl.ds(t, 1)], xg_hbm.at[pl.ds(dest_ref[k, t], 1)], sem)

    def start(t, c):
        for k in range(TOP_K):
            copy(t, k).start()
        return c

    def wait(t, c):
        for k in range(TOP_K):
            copy(t, k).wait()
        return c

    lax.fori_loop(0, tm, start, 0, unroll=4)
    lax.fori_loop(0, tm, wait, 0, unroll=4)


def _dispatch(zflag, dest, xn):
    T, D = xn.shape
    tm = min(DISPATCH_TM, T)
    n_chunks = zflag.shape[0]
    return pl.pallas_call(
        functools.partial(_dispatch_kernel, tm=tm, rows=ZERO_ROWS, n_tiles=n_chunks),
        grid_spec=pltpu.PrefetchScalarGridSpec(
            num_scalar_prefetch=1,
            grid=(T // tm,),
            in_specs=[pl.BlockSpec((8, tm), lambda i, zf: (0, i), memory_space=pltpu.SMEM),
                      pl.BlockSpec((tm, D), lambda i, zf: (i, 0))],
            out_specs=pl.BlockSpec(memory_space=pl.ANY),
            scratch_shapes=[pltpu.VMEM((ZERO_ROWS, D), F32), pltpu.SemaphoreType.DMA]),
        out_shape=SDS((n_chunks * ZERO_ROWS, D), F32),
        compiler_params=_params(1),
        name="dispatch",
    )(zflag, dest, xn)


def _row_blocks(valid, n_rows, compute, out_ref):
    full = valid > n_rows - EXPERT_SUB

    @pl.when(full)
    def _():
        compute(slice(0, n_rows))

    for sb in range(n_rows // EXPERT_SUB):
        rows = slice(sb * EXPERT_SUB, (sb + 1) * EXPERT_SUB)

        @pl.when(jnp.logical_not(full) & (sb * EXPERT_SUB < valid))
        def _():
            compute(rows)

        @pl.when(jnp.logical_not(full) & (sb * EXPERT_SUB >= valid))
        def _():
            out_ref[rows, :] = jnp.zeros((EXPERT_SUB, out_ref.shape[1]), out_ref.dtype)


def _ffn_up_kernel(te_ref, tv_ref, nu_ref, x_ref, wg_ref, wu_ref, bg_ref, bu_ref, h_ref, wgb_ref, wub_ref):
    i = pl.program_id(1)
    valid = tv_ref[i]
    new_expert = (i == 0) | (te_ref[i] != te_ref[jnp.maximum(i - 1, 0)])

    @pl.when((valid > 0) & new_expert)
    def _():
        wgb_ref[...] = wg_ref[...].astype(BF16)
        wub_ref[...] = wu_ref[...].astype(BF16)

    def compute(rows):
        x = x_ref[rows, :].astype(BF16)
        g = _dot(x, wgb_ref[...]) + bg_ref[...]
        u = _dot(x, wub_ref[...]) + bu_ref[...]
        g = jnp.minimum(g, SWIGLU_LIMIT)
        u = jnp.clip(u, -SWIGLU_LIMIT, SWIGLU_LIMIT)
        h_ref[rows, :] = (g * jax.nn.sigmoid(SWIGLU_ALPHA * g) * (u + 1.0)).astype(BF16)

    _row_blocks(valid, x_ref.shape[0], compute, h_ref)


def _ffn_up(tile_expert, tile_valid, n_used, xg, wg, wu, bg, bu):
    P, D = xg.shape
    Dx = wg.shape[2]
    tr = EXPERT_ROWS
    tn = min(EXPERT_TN_UP, Dx)
    n_tiles = P // tr
    xmap = lambda j, i, te, tv, nu: (jnp.minimum(i, nu[0] - 1), 0)
    wmap = lambda j, i, te, tv, nu: (te[i], 0, j)
    return pl.pallas_call(
        _ffn_up_kernel,
        grid_spec=pltpu.PrefetchScalarGridSpec(
            num_scalar_prefetch=3,
            grid=(Dx // tn, n_tiles),
            in_specs=[pl.BlockSpec((tr, D), xmap),
                      pl.BlockSpec((None, D, tn), wmap),
                      pl.BlockSpec((None, D, tn), wmap),
                      pl.BlockSpec((None, 1, tn), wmap),
                      pl.BlockSpec((None, 1, tn), wmap)],
            out_specs=pl.BlockSpec((tr, tn), lambda j, i, te, tv, nu: (i, j)),
            scratch_shapes=[pltpu.VMEM((D, tn), BF16), pltpu.VMEM((D, tn), BF16)]),
        out_shape=SDS((P, Dx), BF16),
        compiler_params=_params(2),
        name="ffn_up",
    )(tile_expert, tile_valid, n_used, xg, wg, wu, bg, bu)


def _ffn_down_kernel(te_ref, tv_ref, nu_ref, h_ref, wd_ref, bd_ref, y_ref, wdb_ref):
    i = pl.program_id(1)
    valid = tv_ref[i]
    new_expert = (i == 0) | (te_ref[i] != te_ref[jnp.maximum(i - 1, 0)])

    @pl.when((valid > 0) & new_expert)
    def _():
        wdb_ref[...] = wd_ref[...].astype(BF16)

    def compute(rows):
        y_ref[rows, :] = _dot(h_ref[rows, :], wdb_ref[...]) + bd_ref[...]

    _row_blocks(valid, h_ref.shape[0], compute, y_ref)


def _ffn_down(tile_expert, tile_valid, n_used, hid, wd, bd):
    P, Dx = hid.shape
    D = wd.shape[2]
    tr = EXPERT_ROWS
    tn = min(EXPERT_TN_DOWN, D)
    n_tiles = P // tr
    hmap = lambda j, i, te, tv, nu: (jnp.minimum(i, nu[0] - 1), 0)
    wmap = lambda j, i, te, tv, nu: (te[i], 0, j)
    return pl.pallas_call(
        _ffn_down_kernel,
        grid_spec=pltpu.PrefetchScalarGridSpec(
            num_scalar_prefetch=3,
            grid=(D // tn, n_tiles),
            in_specs=[pl.BlockSpec((tr, Dx), hmap),
                      pl.BlockSpec((None, Dx, tn), wmap),
                      pl.BlockSpec((None, 1, tn), wmap)],
            out_specs=pl.BlockSpec((tr, tn), lambda j, i, te, tv, nu: (i, j)),
            scratch_shapes=[pltpu.VMEM((Dx, tn), BF16)]),
        out_shape=SDS((P, D), F32),
        compiler_params=_params(2),
        name="ffn_down",
    )(tile_expert, tile_valid, n_used, hid, wd, bd)


def _combine_kernel(dest_ref, w_ref, h1_ref, p_ref, wple_ref, wpg_ref, pg_ref, fg_ref, y_hbm,
                    o_ref, ybuf_ref, sem, *, tm, final):
    def copy(t, k):
        return pltpu.make_async_copy(y_hbm.at[pl.ds(dest_ref[k, t], 1)],
                                     ybuf_ref.at[k, pl.ds(t, 1)], sem)

    def start(t, c):
        for k in range(TOP_K):
            copy(t, k).start()
        return c

    def wait(t, c):
        for k in range(TOP_K):
            copy(t, k).wait()
        return c

    lax.fori_loop(0, tm, start, 0, unroll=4)
    pw = _dot(p_ref[...].astype(BF16), wple_ref[...])
    lax.fori_loop(0, tm, wait, 0, unroll=4)
    w = w_ref[...]
    moe = w[:, 0:1] * ybuf_ref[0]
    for k in range(1, TOP_K):
        moe = moe + w[:, k:k + 1] * ybuf_ref[k]
    h2 = h1_ref[...] + moe
    xn = _rms(h2, pg_ref[...]).astype(BF16)
    gate = jax.nn.sigmoid(_dot(xn, wpg_ref[...]))
    h3 = h2 + pw * gate
    o_ref[...] = _rms(h3, fg_ref[...]) if final else h3


def _combine(dest, wts, h1, p2, wple, wpg, pgain, fgain, yg, final):
    T, D = h1.shape
    tm = min(COMBINE_TM, T)
    row = lambda a: pl.BlockSpec((tm, a.shape[1]), lambda i: (i, 0))
    return pl.pallas_call(
        functools.partial(_combine_kernel, tm=tm, final=final),
        grid=(T // tm,),
        in_specs=[pl.BlockSpec((8, tm), lambda i: (0, i), memory_space=pltpu.SMEM),
                  row(wts), row(h1), row(p2),
                  _resident(wple), _resident(wpg), _resident(pgain), _resident(fgain),
                  pl.BlockSpec(memory_space=pl.ANY)],
        out_specs=pl.BlockSpec((tm, D), lambda i: (i, 0)),
        out_shape=SDS((T, D), F32),
        scratch_shapes=[pltpu.VMEM((TOP_K, tm, D), F32), pltpu.SemaphoreType.DMA],
        compiler_params=_params(1),
        name="combine",
    )(dest, wts, h1, p2, wple, wpg, pgain, fgain, yg)


def kernel(x, p, attn_norm, w_in, q_lat_norm, kv_lat_norm, w_uq, w_ukv, w_o_mla, w_o_moba, w_out,
           rel_bias, moe_norm, w_router, b_router, w_gate, b_gate, w_up, b_up, w_down, b_down,
           ple_norm, w_ple_gate, w_ple, final_norm):
    B, S, D = x.shape
    T = B * S
    n_layers = w_in.shape[0]
    E = w_router.shape[-1]
    H = MLA_HEADS
    mw = MOBA_HEADS * MOBA_HEAD_DIM
    assert S % MOBA_BLOCK == 0 and E <= LANES
    o_kr = MLA_Q_LORA + MLA_KV_LORA
    o_q = o_kr + MLA_ROPE
    o_g = o_q + 3 * mw

    inv = 1.0 / (ROPE_THETA ** (jnp.arange(0, MLA_ROPE, 2, dtype=F32) / MLA_ROPE))
    ang = jnp.arange(S, dtype=F32)[:, None] * inv[None, :]
    cos, sin = jnp.cos(ang), jnp.sin(ang)
    zpad = jnp.zeros((S, LANES - MLA_ROPE), F32)
    cosw = jnp.concatenate([cos, cos, zpad], axis=1)
    sinw = jnp.concatenate([-sin, sin, zpad], axis=1)
    r = jnp.arange(MOBA_BLOCK)
    d0 = r[None, :] - r[:, None]
    bidx = jnp.stack([_t5_bucket(d0), _t5_bucket(d0 + MOBA_BLOCK)]).astype(I32)
    et = (jnp.arange(S)[:, None] // MOBA_BLOCK == jnp.arange(LANES)[None, :]).astype(BF16)
    bias = _moba_bias(rel_bias, bidx)

    h = x.reshape(T, D)
    for li in range(n_layers):
        w = w_in[li]
        w_main = jnp.concatenate([w[:, :o_kr], w[:, o_q:]], axis=1).astype(BF16)
        w_kr = jnp.pad(w[:, o_kr:o_q], ((0, 0), (0, LANES - MLA_ROPE))).astype(BF16)
        wq = w_uq[li].reshape(MLA_Q_LORA, H, MLA_NOPE + MLA_ROPE)
        wq = jnp.pad(wq, ((0, 0), (0, 0), (0, MLA_SLOT - MLA_NOPE - MLA_ROPE)))
        wq = wq.reshape(MLA_Q_LORA, H * MLA_SLOT).astype(BF16)
        wkv = w_ukv[li].reshape(MLA_KV_LORA, H, MLA_NOPE + MLA_V)
        wk = wkv[:, :, :MLA_NOPE].reshape(MLA_KV_LORA, H * MLA_NOPE).astype(BF16)
        wv = wkv[:, :, MLA_NOPE:].reshape(MLA_KV_LORA, H * MLA_V).astype(BF16)
        wr = jnp.pad(w_router[li], ((0, 0), (0, LANES - E))).astype(BF16)
        br = jnp.pad(b_router[li], (0, LANES - E), constant_values=NEG_INF)[None, :]

        lat, qkv, gates, kr = _in_proj(h, attn_norm[li][None, :], w_main, w_kr, o_kr, 3 * mw)
        q_a, k_a, v_a = _mla_proj(lat, kr, cosw, sinw, q_lat_norm[li][None, :], kv_lat_norm[li][None, :],
                                  wq, wk, wv, S)
        y_a = _mla_attn(q_a, k_a, v_a, B, S)
        y_b = _moba_attn(qkv, et, bias, rel_bias, B, S)
        h1, xn, logits = _out_proj(y_a, y_b, gates, h, w_o_mla[li].astype(BF16), w_o_moba[li].astype(BF16),
                                   w_out[li].astype(BF16), moe_norm[li][None, :], wr, br)

        e_k, w_k, r_k, cnt = _route(logits, E)
        counts = cnt[:E, 0].astype(I32)
        tiles = (counts + EXPERT_ROWS - 1) // EXPERT_ROWS
        tile_end = jnp.cumsum(tiles)
        tile_start = tile_end - tiles
        n_tiles = (T * TOP_K) // EXPERT_ROWS + E
        n_used = tile_end[-1]
        tile_ids = jnp.arange(n_tiles)
        capped = jnp.minimum(tile_ids, n_used - 1)
        tile_expert = jnp.minimum(jnp.sum(tile_end[None, :] <= capped[:, None], axis=1), E - 1).astype(I32)
        base = jnp.pad(tile_start * EXPERT_ROWS, (0, LANES - E)).astype(I32)
        dest = _dest(e_k, r_k, jnp.broadcast_to(base[:, None], (LANES, LANES)))
        mine = tile_expert[:, None] == jnp.arange(E)[None, :]
        in_tile = jnp.sum(jnp.where(mine, counts[None, :] - (tile_ids[:, None] - tile_start[None, :]) * EXPERT_ROWS, 0),
                          axis=1)
        tile_valid = jnp.where(tile_ids < n_used, jnp.clip(in_tile, 0, EXPERT_ROWS), 0).astype(I32)
        chunk_lo = jnp.arange(n_tiles * EXPERT_ROWS // ZERO_ROWS) * ZERO_ROWS
        pad_lo = tile_start * EXPERT_ROWS + counts
        pad_hi = tile_end * EXPERT_ROWS
        in_pad = (chunk_lo[:, None] < pad_hi[None, :]) & (chunk_lo[:, None] + ZERO_ROWS > pad_lo[None, :])
        zflag = (jnp.any(in_pad, axis=1) | (chunk_lo >= n_used * EXPERT_ROWS)).astype(I32)
        nu = n_used.astype(I32)[None]

        xg = _dispatch(zflag, dest, xn)
        hid = _ffn_up(tile_expert, tile_valid, nu, xg, w_gate[li], w_up[li],
                      b_gate[li][:, None, :], b_up[li][:, None, :])
        yg = _ffn_down(tile_expert, tile_valid, nu, hid, w_down[li], b_down[li][:, None, :])
        h = _combine(dest, w_k[:TOP_K].T, h1, p[li].reshape(T, -1), w_ple[li].astype(BF16),
                     w_ple_gate[li].astype(BF16), ple_norm[li][None, :], final_norm[None, :], yg,
                     final=li == n_layers - 1)
    return h.reshape(B, S, D)
```

```python
import functools
import math

import jax
import jax.numpy as jnp
from jax import lax
from jax.experimental import pallas as pl
from jax.experimental.pallas import tpu as pltpu

F32 = jnp.float32
BF16 = jnp.bfloat16
I32 = jnp.int32
U32 = jnp.uint32
HIGH_HALF = 0xFFFF0000
SDS = jax.ShapeDtypeStruct

EPS = 1e-6
NEG_INF = -1e30
MLA_HEADS = 8
MLA_NOPE = 128
MLA_ROPE = 64
MLA_V = 128
MLA_Q_LORA = 512
MLA_KV_LORA = 512
ROPE_THETA = 10000.0
MOBA_HEADS = 8
MOBA_HEAD_DIM = 128
MOBA_BLOCK = 256
MOBA_TOPK = 3
REL_BUCKETS = 32
REL_MAX_DIST = 128
TOP_K = 4
SWIGLU_LIMIT = 7.0
SWIGLU_ALPHA = 1.702

LANES = 128
MLA_SLOT = 2 * LANES
VMEM_LIMIT = 56 * 2**20

IN_TM, IN_TN = 512, 1024
MLAP_TM = 512
MLA_TQ = 512
MLA_TK = 512
MLA_HEADS_PER_STEP = 4
MOBA_HEADS_PER_STEP = 4
OUT_TM = 256
ROUTE_TM = 512
EXPERT_ROWS = 1024
EXPERT_SUB = 256
EXPERT_TN_UP = 512
EXPERT_TN_DOWN = 1024
ZERO_ROWS = 512
DISPATCH_TM = 512
DEST_TB = 2048
COMBINE_TM = 256

_NT = (((1,), (1,)), ((), ()))


def _params(n_axes):
    return pltpu.CompilerParams(dimension_semantics=("arbitrary",) * n_axes,
                                vmem_limit_bytes=VMEM_LIMIT)


def _rms(x, g):
    return x * lax.rsqrt(jnp.mean(x * x, axis=-1, keepdims=True) + EPS) * g


def _dot(a, b):
    return jnp.dot(a, b, preferred_element_type=F32)


def _in_proj_kernel(x_ref, g_ref, w_ref, wkr_ref, lat_ref, qkv_ref, gates_ref, kr_ref, xn_ref,
                    *, n_lat, n_qkv):
    j = pl.program_id(1)

    @pl.when(j == 0)
    def _():
        xn = _rms(x_ref[...], g_ref[...]).astype(BF16)
        xn_ref[...] = xn
        kr_ref[...] = _dot(xn, wkr_ref[...])

    acc = _dot(xn_ref[...], w_ref[...])

    @pl.when(j < n_lat)
    def _():
        lat_ref[...] = acc

    @pl.when((j >= n_lat) & (j < n_lat + n_qkv))
    def _():
        qkv_ref[...] = acc.astype(BF16)

    @pl.when(j >= n_lat + n_qkv)
    def _():
        gates_ref[...] = acc


def _in_proj(x2, gain, w_main, w_kr, n_lat_cols, n_qkv_cols):
    T, D = x2.shape
    n_g_cols = w_main.shape[1] - n_lat_cols - n_qkv_cols
    tm = min(IN_TM, T)
    tn = math.gcd(math.gcd(IN_TN, n_lat_cols), math.gcd(n_qkv_cols, n_g_cols))
    n_lat, n_qkv, n_g = n_lat_cols // tn, n_qkv_cols // tn, n_g_cols // tn
    return pl.pallas_call(
        functools.partial(_in_proj_kernel, n_lat=n_lat, n_qkv=n_qkv),
        grid=(T // tm, n_lat + n_qkv + n_g),
        in_specs=[pl.BlockSpec((tm, D), lambda i, j: (i, 0)),
                  pl.BlockSpec((1, D), lambda i, j: (0, 0)),
                  pl.BlockSpec((D, tn), lambda i, j: (0, j)),
                  pl.BlockSpec((D, LANES), lambda i, j: (0, 0))],
        out_specs=[pl.BlockSpec((tm, tn), lambda i, j: (i, jnp.minimum(j, n_lat - 1))),
                   pl.BlockSpec((tm, tn), lambda i, j: (i, jnp.clip(j - n_lat, 0, n_qkv - 1))),
                   pl.BlockSpec((tm, tn), lambda i, j: (i, jnp.maximum(j - n_lat - n_qkv, 0))),
                   pl.BlockSpec((tm, LANES), lambda i, j: (i, 0))],
        out_shape=[SDS((T, n_lat_cols), F32), SDS((T, n_qkv_cols), BF16),
                   SDS((T, n_g_cols), F32), SDS((T, LANES), F32)],
        scratch_shapes=[pltpu.VMEM((tm, D), BF16)],
        compiler_params=_params(2),
        name="in_proj",
    )(x2, gain, w_main, w_kr)


def _mla_proj_kernel(lat_ref, kr_ref, cos_ref, sin_ref, qn_ref, kvn_ref, wq_ref, wk_ref, wv_ref,
                     q_ref, k_ref, v_ref):
    lat = lat_ref[...]
    qn = _rms(lat[:, :MLA_Q_LORA], qn_ref[...]).astype(BF16)
    kvn = _rms(lat[:, MLA_Q_LORA:], kvn_ref[...]).astype(BF16)
    q = _dot(qn, wq_ref[...])
    kn = _dot(kvn, wk_ref[...])
    v_ref[...] = _dot(kvn, wv_ref[...]).astype(BF16)
    c = cos_ref[...]
    s = sin_ref[...]
    half = MLA_ROPE // 2
    lane = lax.broadcasted_iota(I32, c.shape, 1)

    def rope(xr):
        swapped = jnp.where(lane < half, pltpu.roll(xr, LANES - half, 1), pltpu.roll(xr, half, 1))
        return xr * c + swapped * s

    kr = rope(kr_ref[...]).astype(BF16)
    for h in range(MLA_HEADS):
        lo = h * MLA_SLOT
        q_ref[:, lo:lo + LANES] = q[:, lo:lo + LANES].astype(BF16)
        q_ref[:, lo + LANES:lo + MLA_SLOT] = rope(q[:, lo + LANES:lo + MLA_SLOT]).astype(BF16)
        k_ref[:, lo:lo + LANES] = kn[:, h * MLA_NOPE:(h + 1) * MLA_NOPE].astype(BF16)
        k_ref[:, lo + LANES:lo + MLA_SLOT] = kr


def _mla_proj(lat, kr, cosw, sinw, qnorm, kvnorm, wq, wk, wv, S):
    T = lat.shape[0]
    tm = min(MLAP_TM, S)
    ns = S // tm
    H = MLA_HEADS
    full = lambda a: pl.BlockSpec(a.shape, lambda i: (0,) * a.ndim)
    return pl.pallas_call(
        _mla_proj_kernel,
        grid=(T // tm,),
        in_specs=[pl.BlockSpec((tm, lat.shape[1]), lambda i: (i, 0)),
                  pl.BlockSpec((tm, LANES), lambda i: (i, 0)),
                  pl.BlockSpec((tm, LANES), lambda i: (i % ns, 0)),
                  pl.BlockSpec((tm, LANES), lambda i: (i % ns, 0)),
                  full(qnorm), full(kvnorm), full(wq), full(wk), full(wv)],
        out_specs=[pl.BlockSpec((tm, H * MLA_SLOT), lambda i: (i, 0)),
                   pl.BlockSpec((tm, H * MLA_SLOT), lambda i: (i, 0)),
                   pl.BlockSpec((tm, H * MLA_V), lambda i: (i, 0))],
        out_shape=[SDS((T, H * MLA_SLOT), BF16), SDS((T, H * MLA_SLOT), BF16),
                   SDS((T, H * MLA_V), BF16)],
        compiler_params=_params(1),
        name="mla_proj",
    )(lat, kr, cosw, sinw, qnorm, kvnorm, wq, wk, wv)


def _transpose_chunks(src_ref, dst_ref):
    n, _, tk = dst_ref.shape
    for c in range(n):
        dst_ref[c] = src_ref[c * tk:(c + 1) * tk, :].astype(F32).T.astype(dst_ref.dtype)


def _online_softmax_t(carry, s, vt):
    return _online_softmax_heads((carry,), (s,), (vt,))[0]


def _online_softmax_heads(carries, ss, vts):
    stats = []
    for (m, l, acc), s in zip(carries, ss):
        m_new = jnp.maximum(m, jnp.max(s, axis=0, keepdims=True))
        alpha = jnp.exp(m - m_new)
        p = jnp.exp(s - m_new)
        stats.append((m_new, alpha * l + jnp.sum(p, axis=0, keepdims=True), alpha, p.astype(BF16)))
    return tuple((m_new, l, alpha * acc + _dot(vt, p))
                 for (m_new, l, alpha, p), (_, _, acc), vt in zip(stats, carries, vts))


def _mla_attn_kernel(q_ref, k_ref, v_ref, o_ref, vt_ref, *, tq, tk, scale, heads):
    qi = pl.program_id(2)
    slot, dv = MLA_SLOT, MLA_V

    @pl.when(qi == 0)
    def _():
        for g in range(heads):
            _transpose_chunks(v_ref.at[:, g * dv:(g + 1) * dv], vt_ref.at[g])

    qs = [q_ref[:, g * slot:(g + 1) * slot] for g in range(heads)]
    per_q = tq // tk

    def chunk(g, c):
        k = k_ref[pl.ds(pl.multiple_of(c * tk, tk), tk), g * slot:(g + 1) * slot]
        return lax.dot_general(k, qs[g], _NT, preferred_element_type=F32) * scale, vt_ref[g, c]

    carry = [None] * heads
    for c in range(per_q):
        for g in range(heads):
            s, vt = chunk(g, qi * per_q + c)
            key = lax.broadcasted_iota(I32, s.shape, 0) + c * tk
            qry = lax.broadcasted_iota(I32, s.shape, 1)
            s = jnp.where(key <= qry, s, NEG_INF)
            if carry[g] is None:
                m = jnp.max(s, axis=0, keepdims=True)
                p = jnp.exp(s - m)
                carry[g] = (m, jnp.sum(p, axis=0, keepdims=True), _dot(vt, p.astype(BF16)))
            else:
                carry[g] = _online_softmax_t(carry[g], s, vt)

    def body(c, carry):
        ss, vts = zip(*[chunk(g, c) for g in range(heads)])
        return _online_softmax_heads(carry, ss, vts)

    carry = lax.fori_loop(0, qi * per_q, body, tuple(carry))
    for g in range(heads):
        m, l, acc = carry[g]
        o_ref[:, g * dv:(g + 1) * dv] = (acc / l).T.astype(BF16)


def _mla_attn(q, k, v, B, S):
    H = MLA_HEADS
    G = MLA_HEADS_PER_STEP
    T = B * S
    tq = min(MLA_TQ, S)
    tk = min(MLA_TK, tq)
    nq = S // tq
    scale = (MLA_NOPE + MLA_ROPE) ** -0.5
    return pl.pallas_call(
        functools.partial(_mla_attn_kernel, tq=tq, tk=tk, scale=scale, heads=G),
        grid=(B, H // G, nq),
        in_specs=[pl.BlockSpec((tq, G * MLA_SLOT), lambda b, h, i: (b * nq + i, h)),
                  pl.BlockSpec((S, G * MLA_SLOT), lambda b, h, i: (b, h)),
                  pl.BlockSpec((S, G * MLA_V), lambda b, h, i: (b, h))],
        out_specs=pl.BlockSpec((tq, G * MLA_V), lambda b, h, i: (b * nq + i, h)),
        out_shape=SDS((T, H * MLA_V), BF16),
        scratch_shapes=[pltpu.VMEM((G, S // tk, MLA_V, tk), BF16)],
        compiler_params=_params(3),
        name="mla_attn",
    )(q, k, v)


def _t5_bucket(dist):
    n = jnp.maximum(dist, 0)
    max_exact = REL_BUCKETS // 2
    large = max_exact + (jnp.log(jnp.maximum(n, 1).astype(F32) / max_exact)
                         / math.log(REL_MAX_DIST / max_exact)
                         * (REL_BUCKETS - max_exact)).astype(I32)
    large = jnp.minimum(large, REL_BUCKETS - 1)
    return jnp.where(n < max_exact, n, large)


def _moba_bias_kernel(rb_ref, bidx_ref, o_ref):
    h = pl.program_id(0)
    for t in range(2):
        bi = bidx_ref[t]
        val = jnp.zeros(bi.shape, F32)
        for b in range(REL_BUCKETS):
            val = jnp.where(bi == b, rb_ref[b, h], val)
        o_ref[t] = val


def _moba_bias(rel_bias, bidx):
    H = rel_bias.shape[1]
    blk = MOBA_BLOCK
    return pl.pallas_call(
        _moba_bias_kernel,
        grid=(H,),
        in_specs=[pl.BlockSpec(memory_space=pltpu.SMEM),
                  pl.BlockSpec((2, blk, blk), lambda h: (0, 0, 0))],
        out_specs=pl.BlockSpec((None, 2, blk, blk), lambda h: (h, 0, 0, 0)),
        out_shape=SDS((H, 2, blk, blk), F32),
        compiler_params=_params(1),
        name="moba_bias",
    )(rel_bias, bidx)


def _moba_attn_kernel(rb_ref, q_ref, k_ref, v_ref, et_ref, bias_ref, o_ref, ka_ref, km_ref, qa_ref, vt_ref,
                      *, nb, n_sel, scale, heads):
    hg = pl.program_id(1)
    i = pl.program_id(2)
    blk = MOBA_BLOCK
    d = MOBA_HEAD_DIM
    nbp = -(-nb // 8) * 8

    @pl.when(i == 0)
    def _():
        for g in range(heads):
            cols = slice(g * d, (g + 1) * d)
            ka_ref[g, :, :d] = k_ref[:, cols]
            ka_ref[g, :, d:] = et_ref[...]
            _transpose_chunks(v_ref.at[:, cols], vt_ref.at[g])
            km_ref[g] = jnp.zeros(km_ref.shape[1:], F32)
            for n in range(nb):
                kb = k_ref[n * blk:(n + 1) * blk, cols].astype(F32)
                km_ref[g, n:n + 1, :] = jnp.sum(kb, axis=0, keepdims=True) * (1.0 / blk)

    qas = []
    for g in range(heads):
        q = q_ref[:, g * d:(g + 1) * d]
        gate = lax.dot_general(km_ref[g, :nbp, :].astype(BF16), q, _NT, preferred_element_type=F32)
        sub = lax.broadcasted_iota(I32, gate.shape, 0)
        gt = jnp.where(sub < i, gate, NEG_INF)
        keep = jnp.full(gate.shape, NEG_INF, F32)
        for _ in range(n_sel):
            mx = jnp.max(gt, axis=0, keepdims=True)
            first = jnp.min(jnp.where(gt == mx, sub, LANES), axis=0, keepdims=True)
            pick = sub == first
            keep = jnp.where(pick & (sub < i), 0.0, keep)
            gt = jnp.where(pick, -3.0e38, gt)
        keep = jnp.where(sub == i, 0.0, keep)
        keep = jnp.concatenate([keep, jnp.full((LANES - nbp, blk), NEG_INF, F32)], axis=0)
        qa_ref[g, :, :d] = q
        qa_ref[g, :, d:] = keep.T.astype(BF16)
        qas.append(qa_ref[g])

    def block(g, n):
        kk = ka_ref[g, pl.ds(pl.multiple_of(n * blk, blk), blk), :]
        return lax.dot_general(kk, qas[g], _NT, preferred_element_type=F32) * scale, vt_ref[g, n]

    carry = []
    for g in range(heads):
        s, vt = block(g, i)
        s = s + bias_ref[g, 0]
        key = lax.broadcasted_iota(I32, s.shape, 0)
        qry = lax.broadcasted_iota(I32, s.shape, 1)
        s = jnp.where(key <= qry, s, NEG_INF)
        m = jnp.max(s, axis=0, keepdims=True)
        p = jnp.exp(s - m)
        carry.append((m, jnp.sum(p, axis=0, keepdims=True), _dot(vt, p.astype(BF16))))

    def adjacent(n, carry):
        ss, vts = zip(*[block(g, n) for g in range(heads)])
        ss = [s + bias_ref[g, 1] for g, s in enumerate(ss)]
        return _online_softmax_heads(carry, ss, vts)

    far_bias = [rb_ref[REL_BUCKETS - 1, hg * heads + g] for g in range(heads)]

    def far(n, carry):
        ss, vts = zip(*[block(g, n) for g in range(heads)])
        ss = [s + far_bias[g] for g, s in enumerate(ss)]
        return _online_softmax_heads(carry, ss, vts)

    n_far = jnp.maximum(i - 1, 0)
    carry = lax.fori_loop(n_far, i, adjacent, tuple(carry))
    carry = lax.fori_loop(0, n_far, far, carry)
    for g in range(heads):
        m, l, acc = carry[g]
        o_ref[:, g * d:(g + 1) * d] = (acc / l).T.astype(BF16)


def _moba_attn(qkv, et, bias, rel_bias, B, S):
    H, d, blk = MOBA_HEADS, MOBA_HEAD_DIM, MOBA_BLOCK
    T = B * S
    nb = S // blk
    n_sel = min(MOBA_TOPK, nb)
    G = MOBA_HEADS_PER_STEP
    ng = H // G
    return pl.pallas_call(
        functools.partial(_moba_attn_kernel, nb=nb, n_sel=n_sel, scale=d ** -0.5, heads=G),
        grid=(B, ng, nb),
        in_specs=[pl.BlockSpec(memory_space=pltpu.SMEM),
                  pl.BlockSpec((blk, G * d), lambda b, h, i: (b * nb + i, h)),
                  pl.BlockSpec((S, G * d), lambda b, h, i: (b, ng + h)),
                  pl.BlockSpec((S, G * d), lambda b, h, i: (b, 2 * ng + h)),
                  pl.BlockSpec((S, LANES), lambda b, h, i: (0, 0)),
                  pl.BlockSpec((G, 2, blk, blk), lambda b, h, i: (h, 0, 0, 0))],
        out_specs=pl.BlockSpec((blk, G * d), lambda b, h, i: (b * nb + i, h)),
        out_shape=SDS((T, H * d), BF16),
        scratch_shapes=[pltpu.VMEM((G, S, d + LANES), BF16),
                        pltpu.VMEM((G, LANES, d), F32),
                        pltpu.VMEM((G, blk, d + LANES), BF16),
                        pltpu.VMEM((G, nb, d, blk), BF16)],
        compiler_params=_params(3),
        name="moba_attn",
    )(rel_bias, qkv, qkv, qkv, et, bias)


def _out_proj_kernel(ya_ref, yb_ref, g_ref, x_ref, woa_ref, wob_ref, wout_ref, mg_ref, wr_ref, br_ref,
                     h1_ref, xn_ref, lg_ref, *, D):
    a = _dot(ya_ref[...], woa_ref[...])
    b = _dot(yb_ref[...], wob_ref[...])
    g = g_ref[...]
    merged = jax.nn.sigmoid(g[:, :D]) * a + jax.nn.sigmoid(g[:, D:]) * b
    h1 = x_ref[...] + _dot(merged.astype(BF16), wout_ref[...])
    h1_ref[...] = h1
    xn = _rms(h1, mg_ref[...]).astype(BF16)
    lg_ref[...] = _dot(xn, wr_ref[...]) + br_ref[...]
    xn_ref[...] = _pack_bf16_pairs(xn)


def _pack_bf16_pairs(x):
    bits = lax.bitcast_convert_type(x.astype(F32), U32)
    half = x.shape[1] // 2
    return (bits[:, :half] >> 16) | (bits[:, half:] & U32(HIGH_HALF))


def _unpack_bf16_pairs(w):
    lo = lax.bitcast_convert_type(w << 16, F32).astype(BF16)
    hi = lax.bitcast_convert_type(w & U32(HIGH_HALF), F32).astype(BF16)
    return lo, hi


def _resident(a):
    return pl.BlockSpec(a.shape, lambda i: (0,) * a.ndim, pipeline_mode=pl.Buffered(1))


def _out_proj(ya, yb, gates, x2, woa, wob, wout, mgain, wr, br):
    T, D = x2.shape
    tm = min(OUT_TM, T)
    row = lambda a: pl.BlockSpec((tm, a.shape[1]), lambda i: (i, 0))
    return pl.pallas_call(
        functools.partial(_out_proj_kernel, D=D),
        grid=(T // tm,),
        in_specs=[row(ya), row(yb), row(gates), row(x2),
                  _resident(woa), _resident(wob), _resident(wout), _resident(mgain),
                  _resident(wr), _resident(br)],
        out_specs=[pl.BlockSpec((tm, D), lambda i: (i, 0)),
                   pl.BlockSpec((tm, D // 2), lambda i: (i, 0)),
                   pl.BlockSpec((tm, LANES), lambda i: (i, 0))],
        out_shape=[SDS((T, D), F32), SDS((T, D // 2), U32), SDS((T, LANES), F32)],
        compiler_params=_params(1),
        name="out_proj",
    )(ya, yb, gates, x2, woa, wob, wout, mgain, wr, br)


def _route_kernel(lg_ref, e_ref, w_ref, r_ref, cnt_ref, carry_ref, *, n_exp):
    i = pl.program_id(0)

    @pl.when(i == 0)
    def _():
        carry_ref[...] = jnp.zeros(carry_ref.shape, F32)

    lt = lg_ref[...].T
    tm = lt.shape[1]
    sub = lax.broadcasted_iota(I32, lt.shape, 0)
    neg = -jnp.inf
    cur = jnp.where(sub < n_exp, lt, neg)
    vals, idxs = [], []
    for _ in range(TOP_K):
        mx = jnp.max(cur, axis=0, keepdims=True)
        ix = jnp.min(jnp.where(cur == mx, sub, LANES), axis=0, keepdims=True)
        vals.append(mx)
        idxs.append(ix)
        cur = jnp.where(sub == ix, neg, cur)
    ex = [jnp.exp(v - vals[0]) for v in vals]
    den = ex[0]
    for e in ex[1:]:
        den = den + e
    onehot = jnp.zeros(lt.shape, F32)
    for ix in idxs:
        onehot = onehot + (sub == ix).astype(F32)
    r_i = lax.broadcasted_iota(I32, (tm, tm), 0)
    c_i = lax.broadcasted_iota(I32, (tm, tm), 1)
    tri = (r_i <= c_i).astype(BF16)
    incl = _dot(onehot.astype(BF16), tri)
    base = carry_ref[...]
    excl = incl - onehot + base[:, :1]
    e_ref[...] = jnp.zeros(e_ref.shape, I32)
    w_ref[...] = jnp.zeros(w_ref.shape, F32)
    r_ref[...] = jnp.zeros(r_ref.shape, I32)
    for k in range(TOP_K):
        e_ref[k:k + 1, :] = idxs[k]
        w_ref[k:k + 1, :] = ex[k] / den
        rk = jnp.sum(jnp.where(sub == idxs[k], excl, 0.0), axis=0, keepdims=True)
        r_ref[k:k + 1, :] = rk.astype(I32)
    total = base + jnp.sum(onehot, axis=1, keepdims=True)
    carry_ref[...] = total
    cnt_ref[...] = total


def _route(logits, n_exp):
    T = logits.shape[0]
    tm = min(ROUTE_TM, T)
    return pl.pallas_call(
        functools.partial(_route_kernel, n_exp=n_exp),
        grid=(T // tm,),
        in_specs=[pl.BlockSpec((tm, LANES), lambda i: (i, 0))],
        out_specs=[pl.BlockSpec((8, tm), lambda i: (0, i)),
                   pl.BlockSpec((8, tm), lambda i: (0, i)),
                   pl.BlockSpec((8, tm), lambda i: (0, i)),
                   pl.BlockSpec((LANES, LANES), lambda i: (0, 0))],
        out_shape=[SDS((8, T), I32), SDS((8, T), F32), SDS((8, T), I32), SDS((LANES, LANES), F32)],
        scratch_shapes=[pltpu.VMEM((LANES, LANES), F32)],
        compiler_params=_params(1),
        name="route",
    )(logits)


def _dest_kernel(e_ref, r_ref, base_ref, d_ref):
    base = base_ref[...][:, :1]
    sub = lax.broadcasted_iota(I32, (LANES, e_ref.shape[1]), 0)
    d_ref[...] = jnp.zeros(d_ref.shape, I32)
    for k in range(TOP_K):
        off = jnp.sum(jnp.where(sub == e_ref[k:k + 1, :], base, 0), axis=0, keepdims=True)
        d_ref[k:k + 1, :] = r_ref[k:k + 1, :] + off


def _dest(e_k, r_k, base):
    T = e_k.shape[1]
    tb = min(DEST_TB, T)
    blk = pl.BlockSpec((8, tb), lambda i: (0, i))
    return pl.pallas_call(
        _dest_kernel,
        grid=(T // tb,),
        in_specs=[blk, blk, pl.BlockSpec((LANES, LANES), lambda i: (0, 0))],
        out_specs=blk,
        out_shape=SDS((8, T), I32),
        compiler_params=_params(1),
        name="dest",
    )(e_k, r_k, base)


def _dispatch_kernel(zflag_ref, dest_ref, x_ref, xg_hbm, zbuf_ref, sem, *, tm, rows, n_tiles):
    i = pl.program_id(0)

    @pl.when(i == 0)
    def _():
        zbuf_ref[...] = jnp.zeros(zbuf_ref.shape, zbuf_ref.dtype)

        def zcopy(j):
            start = pl.multiple_of(j * rows, rows)
            return pltpu.make_async_copy(zbuf_ref, xg_hbm.at[pl.ds(start, rows)], sem)

        def zstart(j, c):
            @pl.when(zflag_ref[j] != 0)
            def _():
                zcopy(j).start()
            return c

        def zwait(j, c):
            @pl.when(zflag_ref[j] != 0)
            def _():
                zcopy(j).wait()
            return c

        lax.fori_loop(0, n_tiles, zstart, 0)
        lax.fori_loop(0, n_tiles, zwait, 0)

    def copy(t, k):
        return pltpu.make_async_copy(x_ref.at[pl.ds(t, 1)], xg_hbm.at[pl.ds(dest_ref[k, t], 1)], sem)

    def start(t, c):
        for k in range(TOP_K):
            copy(t, k).start()
        return c

    def wait(t, c):
        for k in range(TOP_K):
            copy(t, k).wait()
        return c

    lax.fori_loop(0, tm, start, 0, unroll=4)
    lax.fori_loop(0, tm, wait, 0, unroll=4)


def _dispatch(zflag, dest, xn):
    T, D = xn.shape
    tm = min(DISPATCH_TM, T)
    n_chunks = zflag.shape[0]
    return pl.pallas_call(
        functools.partial(_dispatch_kernel, tm=tm, rows=ZERO_ROWS, n_tiles=n_chunks),
        grid_spec=pltpu.PrefetchScalarGridSpec(
            num_scalar_prefetch=1,
            grid=(T // tm,),
            in_specs=[pl.BlockSpec((8, tm), lambda i, zf: (0, i), memory_space=pltpu.SMEM),
                      pl.BlockSpec((tm, D), lambda i, zf: (i, 0))],
            out_specs=pl.BlockSpec(memory_space=pl.ANY),
            scratch_shapes=[pltpu.VMEM((ZERO_ROWS, D), xn.dtype), pltpu.SemaphoreType.DMA]),
        out_shape=SDS((n_chunks * ZERO_ROWS, D), xn.dtype),
        compiler_params=_params(1),
        name="dispatch",
    )(zflag, dest, xn)


def _row_blocks(valid, n_rows, compute, out_ref):
    full = valid > n_rows - EXPERT_SUB

    @pl.when(full)
    def _():
        compute(slice(0, n_rows))

    for sb in range(n_rows // EXPERT_SUB):
        rows = slice(sb * EXPERT_SUB, (sb + 1) * EXPERT_SUB)

        @pl.when(jnp.logical_not(full) & (sb * EXPERT_SUB < valid))
        def _():
            compute(rows)

        @pl.when(jnp.logical_not(full) & (sb * EXPERT_SUB >= valid))
        def _():
            out_ref[rows, :] = jnp.zeros((EXPERT_SUB, out_ref.shape[1]), out_ref.dtype)


def _ffn_up_kernel(te_ref, tv_ref, nu_ref, x_ref, wg_ref, wu_ref, bg_ref, bu_ref, h_ref, wgb_ref, wub_ref):
    i = pl.program_id(1)
    valid = tv_ref[i]
    new_expert = (i == 0) | (te_ref[i] != te_ref[jnp.maximum(i - 1, 0)])

    @pl.when((valid > 0) & new_expert)
    def _():
        wgb_ref[...] = wg_ref[...].astype(BF16)
        wub_ref[...] = wu_ref[...].astype(BF16)

    half = x_ref.shape[1]

    def compute(rows):
        lo, hi = _unpack_bf16_pairs(x_ref[rows, :])
        g = _dot(lo, wgb_ref[:half, :]) + _dot(hi, wgb_ref[half:, :]) + bg_ref[...]
        u = _dot(lo, wub_ref[:half, :]) + _dot(hi, wub_ref[half:, :]) + bu_ref[...]
        g = jnp.minimum(g, SWIGLU_LIMIT)
        u = jnp.clip(u, -SWIGLU_LIMIT, SWIGLU_LIMIT)
        h_ref[rows, :] = (g * jax.nn.sigmoid(SWIGLU_ALPHA * g) * (u + 1.0)).astype(BF16)

    _row_blocks(valid, x_ref.shape[0], compute, h_ref)


def _ffn_up(tile_expert, tile_valid, n_used, xg, wg, wu, bg, bu):
    P = xg.shape[0]
    D, Dx = wg.shape[1], wg.shape[2]
    tr = EXPERT_ROWS
    tn = min(EXPERT_TN_UP, Dx)
    n_tiles = P // tr
    xmap = lambda j, i, te, tv, nu: (jnp.minimum(i, nu[0] - 1), 0)
    wmap = lambda j, i, te, tv, nu: (te[i], 0, j)
    return pl.pallas_call(
        _ffn_up_kernel,
        grid_spec=pltpu.PrefetchScalarGridSpec(
            num_scalar_prefetch=3,
            grid=(Dx // tn, n_tiles),
            in_specs=[pl.BlockSpec((tr, D // 2), xmap),
                      pl.BlockSpec((None, D, tn), wmap),
                      pl.BlockSpec((None, D, tn), wmap),
                      pl.BlockSpec((None, 1, tn), wmap),
                      pl.BlockSpec((None, 1, tn), wmap)],
            out_specs=pl.BlockSpec((tr, tn), lambda j, i, te, tv, nu: (i, j)),
            scratch_shapes=[pltpu.VMEM((D, tn), BF16), pltpu.VMEM((D, tn), BF16)]),
        out_shape=SDS((P, Dx), BF16),
        compiler_params=_params(2),
        name="ffn_up",
    )(tile_expert, tile_valid, n_used, xg, wg, wu, bg, bu)


def _ffn_down_kernel(te_ref, tv_ref, nu_ref, h_ref, wd_ref, bd_ref, y_ref, wdb_ref):
    i = pl.program_id(1)
    valid = tv_ref[i]
    new_expert = (i == 0) | (te_ref[i] != te_ref[jnp.maximum(i - 1, 0)])

    @pl.when((valid > 0) & new_expert)
    def _():
        wdb_ref[...] = wd_ref[...].astype(BF16)

    def compute(rows):
        y_ref[rows, :] = _dot(h_ref[rows, :], wdb_ref[...]) + bd_ref[...]

    _row_blocks(valid, h_ref.shape[0], compute, y_ref)


def _ffn_down(tile_expert, tile_valid, n_used, hid, wd, bd):
    P, Dx = hid.shape
    D = wd.shape[2]
    tr = EXPERT_ROWS
    tn = min(EXPERT_TN_DOWN, D)
    n_tiles = P // tr
    hmap = lambda j, i, te, tv, nu: (jnp.minimum(i, nu[0] - 1), 0)
    wmap = lambda j, i, te, tv, nu: (te[i], 0, j)
    return pl.pallas_call(
        _ffn_down_kernel,
        grid_spec=pltpu.PrefetchScalarGridSpec(
            num_scalar_prefetch=3,
            grid=(D // tn, n_tiles),
            in_specs=[pl.BlockSpec((tr, Dx), hmap),
                      pl.BlockSpec((None, Dx, tn), wmap),
                      pl.BlockSpec((None, 1, tn), wmap)],
            out_specs=pl.BlockSpec((tr, tn), lambda j, i, te, tv, nu: (i, j)),
            scratch_shapes=[pltpu.VMEM((Dx, tn), BF16)]),
        out_shape=SDS((P, D), F32),
        compiler_params=_params(2),
        name="ffn_down",
    )(tile_expert, tile_valid, n_used, hid, wd, bd)


def _combine_kernel(dest_ref, w_ref, h1_ref, p_ref, wple_ref, wpg_ref, pg_ref, fg_ref, y_hbm,
                    o_ref, ybuf_ref, sem, *, tm, final):
    def copy(t, k):
        return pltpu.make_async_copy(y_hbm.at[pl.ds(dest_ref[k, t], 1)],
                                     ybuf_ref.at[k, pl.ds(t, 1)], sem)

    def start(t, c):
        for k in range(TOP_K):
            copy(t, k).start()
        return c

    def wait(t, c):
        for k in range(TOP_K):
            copy(t, k).wait()
        return c

    lax.fori_loop(0, tm, start, 0, unroll=4)
    pw = _dot(p_ref[...].astype(BF16), wple_ref[...])
    lax.fori_loop(0, tm, wait, 0, unroll=4)
    w = w_ref[...]
    moe = w[:, 0:1] * ybuf_ref[0]
    for k in range(1, TOP_K):
        moe = moe + w[:, k:k + 1] * ybuf_ref[k]
    h2 = h1_ref[...] + moe
    xn = _rms(h2, pg_ref[...]).astype(BF16)
    gate = jax.nn.sigmoid(_dot(xn, wpg_ref[...]))
    h3 = h2 + pw * gate
    o_ref[...] = _rms(h3, fg_ref[...]) if final else h3


def _combine(dest, wts, h1, p2, wple, wpg, pgain, fgain, yg, final):
    T, D = h1.shape
    tm = min(COMBINE_TM, T)
    row = lambda a: pl.BlockSpec((tm, a.shape[1]), lambda i: (i, 0))
    return pl.pallas_call(
        functools.partial(_combine_kernel, tm=tm, final=final),
        grid=(T // tm,),
        in_specs=[pl.BlockSpec((8, tm), lambda i: (0, i), memory_space=pltpu.SMEM),
                  row(wts), row(h1), row(p2),
                  _resident(wple), _resident(wpg), _resident(pgain), _resident(fgain),
                  pl.BlockSpec(memory_space=pl.ANY)],
        out_specs=pl.BlockSpec((tm, D), lambda i: (i, 0)),
        out_shape=SDS((T, D), F32),
        scratch_shapes=[pltpu.VMEM((TOP_K, tm, D), F32), pltpu.SemaphoreType.DMA],
        compiler_params=_params(1),
        name="combine",
    )(dest, wts, h1, p2, wple, wpg, pgain, fgain, yg)


def kernel(x, p, attn_norm, w_in, q_lat_norm, kv_lat_norm, w_uq, w_ukv, w_o_mla, w_o_moba, w_out,
           rel_bias, moe_norm, w_router, b_router, w_gate, b_gate, w_up, b_up, w_down, b_down,
           ple_norm, w_ple_gate, w_ple, final_norm):
    B, S, D = x.shape
    T = B * S
    n_layers = w_in.shape[0]
    E = w_router.shape[-1]
    H = MLA_HEADS
    mw = MOBA_HEADS * MOBA_HEAD_DIM
    assert S % MOBA_BLOCK == 0 and E <= LANES
    o_kr = MLA_Q_LORA + MLA_KV_LORA
    o_q = o_kr + MLA_ROPE
    o_g = o_q + 3 * mw

    inv = 1.0 / (ROPE_THETA ** (jnp.arange(0, MLA_ROPE, 2, dtype=F32) / MLA_ROPE))
    ang = jnp.arange(S, dtype=F32)[:, None] * inv[None, :]
    cos, sin = jnp.cos(ang), jnp.sin(ang)
    zpad = jnp.zeros((S, LANES - MLA_ROPE), F32)
    cosw = jnp.concatenate([cos, cos, zpad], axis=1)
    sinw = jnp.concatenate([-sin, sin, zpad], axis=1)
    r = jnp.arange(MOBA_BLOCK)
    d0 = r[None, :] - r[:, None]
    bidx = jnp.stack([_t5_bucket(d0), _t5_bucket(d0 + MOBA_BLOCK)]).astype(I32)
    et = (jnp.arange(S)[:, None] // MOBA_BLOCK == jnp.arange(LANES)[None, :]).astype(BF16)
    bias = _moba_bias(rel_bias, bidx)

    h = x.reshape(T, D)
    for li in range(n_layers):
        w = w_in[li]
        w_main = jnp.concatenate([w[:, :o_kr], w[:, o_q:]], axis=1).astype(BF16)
        w_kr = jnp.pad(w[:, o_kr:o_q], ((0, 0), (0, LANES - MLA_ROPE))).astype(BF16)
        wq = w_uq[li].reshape(MLA_Q_LORA, H, MLA_NOPE + MLA_ROPE)
        wq = jnp.pad(wq, ((0, 0), (0, 0), (0, MLA_SLOT - MLA_NOPE - MLA_ROPE)))
        wq = wq.reshape(MLA_Q_LORA, H * MLA_SLOT).astype(BF16)
        wkv = w_ukv[li].reshape(MLA_KV_LORA, H, MLA_NOPE + MLA_V)
        wk = wkv[:, :, :MLA_NOPE].reshape(MLA_KV_LORA, H * MLA_NOPE).astype(BF16)
        wv = wkv[:, :, MLA_NOPE:].reshape(MLA_KV_LORA, H * MLA_V).astype(BF16)
        wr = jnp.pad(w_router[li], ((0, 0), (0, LANES - E))).astype(BF16)
        br = jnp.pad(b_router[li], (0, LANES - E), constant_values=NEG_INF)[None, :]

        lat, qkv, gates, kr = _in_proj(h, attn_norm[li][None, :], w_main, w_kr, o_kr, 3 * mw)
        q_a, k_a, v_a = _mla_proj(lat, kr, cosw, sinw, q_lat_norm[li][None, :], kv_lat_norm[li][None, :],
                                  wq, wk, wv, S)
        y_a = _mla_attn(q_a, k_a, v_a, B, S)
        y_b = _moba_attn(qkv, et, bias, rel_bias, B, S)
        h1, xn, logits = _out_proj(y_a, y_b, gates, h, w_o_mla[li].astype(BF16), w_o_moba[li].astype(BF16),
                                   w_out[li].astype(BF16), moe_norm[li][None, :], wr, br)

        e_k, w_k, r_k, cnt = _route(logits, E)
        counts = cnt[:E, 0].astype(I32)
        tiles = (counts + EXPERT_ROWS - 1) // EXPERT_ROWS
        tile_end = jnp.cumsum(tiles)
        tile_start = tile_end - tiles
        n_tiles = (T * TOP_K) // EXPERT_ROWS + E
        n_used = tile_end[-1]
        tile_ids = jnp.arange(n_tiles)
        capped = jnp.minimum(tile_ids, n_used - 1)
        tile_expert = jnp.minimum(jnp.sum(tile_end[None, :] <= capped[:, None], axis=1), E - 1).astype(I32)
        base = jnp.pad(tile_start * EXPERT_ROWS, (0, LANES - E)).astype(I32)
        dest = _dest(e_k, r_k, jnp.broadcast_to(base[:, None], (LANES, LANES)))
        mine = tile_expert[:, None] == jnp.arange(E)[None, :]
        in_tile = jnp.sum(jnp.where(mine, counts[None, :] - (tile_ids[:, None] - tile_start[None, :]) * EXPERT_ROWS, 0),
                          axis=1)
        tile_valid = jnp.where(tile_ids < n_used, jnp.clip(in_tile, 0, EXPERT_ROWS), 0).astype(I32)
        chunk_lo = jnp.arange(n_tiles * EXPERT_ROWS // ZERO_ROWS) * ZERO_ROWS
        pad_lo = tile_start * EXPERT_ROWS + counts
        pad_hi = tile_end * EXPERT_ROWS
        in_pad = (chunk_lo[:, None] < pad_hi[None, :]) & (chunk_lo[:, None] + ZERO_ROWS > pad_lo[None, :])
        zflag = (jnp.any(in_pad, axis=1) | (chunk_lo >= n_used * EXPERT_ROWS)).astype(I32)
        nu = n_used.astype(I32)[None]

        xg = _dispatch(zflag, dest, xn)
        hid = _ffn_up(tile_expert, tile_valid, nu, xg, w_gate[li], w_up[li],
                      b_gate[li][:, None, :], b_up[li][:, None, :])
        yg = _ffn_down(tile_expert, tile_valid, nu, hid, w_down[li], b_down[li][:, None, :])
        h = _combine(dest, w_k[:TOP_K].T, h1, p[li].reshape(T, -1), w_ple[li].astype(BF16),
                     w_ple_gate[li].astype(BF16), ple_norm[li][None, :], final_norm[None, :], yg,
                     final=li == n_layers - 1)
    return h.reshape(B, S, D)
```

```python
import functools
import math

import jax
import jax.numpy as jnp
from jax import lax
from jax.experimental import pallas as pl
from jax.experimental.pallas import tpu as pltpu

F32 = jnp.float32
BF16 = jnp.bfloat16
I32 = jnp.int32
U32 = jnp.uint32
HIGH_HALF = 0xFFFF0000
SDS = jax.ShapeDtypeStruct

EPS = 1e-6
NEG_INF = -1e30
LOG2E = math.log2(math.e)
MLA_HEADS = 8
MLA_NOPE = 128
MLA_ROPE = 64
MLA_V = 128
MLA_Q_LORA = 512
MLA_KV_LORA = 512
ROPE_THETA = 10000.0
MOBA_HEADS = 8
MOBA_HEAD_DIM = 128
MOBA_BLOCK = 256
MOBA_TOPK = 3
REL_BUCKETS = 32
REL_MAX_DIST = 128
TOP_K = 4
SWIGLU_LIMIT = 7.0
SWIGLU_ALPHA = 1.702

LANES = 128
MLA_SLOT = 2 * LANES
VMEM_LIMIT = 56 * 2**20

IN_TM, IN_TN = 512, 1024
MLAP_TM = 512
MLA_TQ = 512
MLA_TK = 512
MLA_HEADS_PER_STEP = 4
MOBA_HEADS_PER_STEP = 4
OUT_TM = 256
ROUTE_TM = 512
EXPERT_ROWS = 1024
EXPERT_SUB = 256
EXPERT_TN_UP = 512
EXPERT_TN_DOWN = 1024
ZERO_ROWS = 512
DISPATCH_TM = 512
DEST_TB = 2048
COMBINE_TM = 512

_NT = (((1,), (1,)), ((), ()))


def _params(n_axes):
    return pltpu.CompilerParams(dimension_semantics=("arbitrary",) * n_axes,
                                vmem_limit_bytes=VMEM_LIMIT)


def _rms(x, g):
    return x * lax.rsqrt(jnp.mean(x * x, axis=-1, keepdims=True) + EPS) * g


def _dot(a, b):
    return jnp.dot(a, b, preferred_element_type=F32)


def _in_proj_kernel(x_ref, g_ref, w_ref, wkr_ref, lat_ref, qkv_ref, gates_ref, kr_ref, xn_ref,
                    *, n_lat, n_qkv):
    j = pl.program_id(1)

    @pl.when(j == 0)
    def _():
        xn = _rms(x_ref[...], g_ref[...]).astype(BF16)
        xn_ref[...] = xn
        kr_ref[...] = _dot(xn, wkr_ref[...])

    acc = _dot(xn_ref[...], w_ref[...])

    @pl.when(j < n_lat)
    def _():
        lat_ref[...] = acc

    @pl.when((j >= n_lat) & (j < n_lat + n_qkv))
    def _():
        qkv_ref[...] = acc.astype(BF16)

    @pl.when(j >= n_lat + n_qkv)
    def _():
        gates_ref[...] = acc


def _in_proj(x2, gain, w_main, w_kr, n_lat_cols, n_qkv_cols):
    T, D = x2.shape
    n_g_cols = w_main.shape[1] - n_lat_cols - n_qkv_cols
    tm = min(IN_TM, T)
    tn = math.gcd(math.gcd(IN_TN, n_lat_cols), math.gcd(n_qkv_cols, n_g_cols))
    n_lat, n_qkv, n_g = n_lat_cols // tn, n_qkv_cols // tn, n_g_cols // tn
    return pl.pallas_call(
        functools.partial(_in_proj_kernel, n_lat=n_lat, n_qkv=n_qkv),
        grid=(T // tm, n_lat + n_qkv + n_g),
        in_specs=[pl.BlockSpec((tm, D), lambda i, j: (i, 0)),
                  pl.BlockSpec((1, D), lambda i, j: (0, 0)),
                  pl.BlockSpec((D, tn), lambda i, j: (0, j)),
                  pl.BlockSpec((D, LANES), lambda i, j: (0, 0))],
        out_specs=[pl.BlockSpec((tm, tn), lambda i, j: (i, jnp.minimum(j, n_lat - 1))),
                   pl.BlockSpec((tm, tn), lambda i, j: (i, jnp.clip(j - n_lat, 0, n_qkv - 1))),
                   pl.BlockSpec((tm, tn), lambda i, j: (i, jnp.maximum(j - n_lat - n_qkv, 0))),
                   pl.BlockSpec((tm, LANES), lambda i, j: (i, 0))],
        out_shape=[SDS((T, n_lat_cols), F32), SDS((T, n_qkv_cols), BF16),
                   SDS((T, n_g_cols), F32), SDS((T, LANES), F32)],
        scratch_shapes=[pltpu.VMEM((tm, D), BF16)],
        compiler_params=_params(2),
        name="in_proj",
    )(x2, gain, w_main, w_kr)


def _mla_proj_kernel(lat_ref, kr_ref, cos_ref, sin_ref, qn_ref, kvn_ref, wq_ref, wk_ref, wv_ref,
                     q_ref, k_ref, v_ref):
    lat = lat_ref[...]
    qn = _rms(lat[:, :MLA_Q_LORA], qn_ref[...]).astype(BF16)
    kvn = _rms(lat[:, MLA_Q_LORA:], kvn_ref[...]).astype(BF16)
    q = _dot(qn, wq_ref[...])
    kn = _dot(kvn, wk_ref[...])
    v_ref[...] = _dot(kvn, wv_ref[...]).astype(BF16)
    c = cos_ref[...]
    s = sin_ref[...]
    half = MLA_ROPE // 2
    lane = lax.broadcasted_iota(I32, c.shape, 1)

    def rope(xr):
        swapped = jnp.where(lane < half, pltpu.roll(xr, LANES - half, 1), pltpu.roll(xr, half, 1))
        return xr * c + swapped * s

    kr = rope(kr_ref[...]).astype(BF16)
    for h in range(MLA_HEADS):
        lo = h * MLA_SLOT
        q_ref[:, lo:lo + LANES] = q[:, lo:lo + LANES].astype(BF16)
        q_ref[:, lo + LANES:lo + MLA_SLOT] = rope(q[:, lo + LANES:lo + MLA_SLOT]).astype(BF16)
        k_ref[:, lo:lo + LANES] = kn[:, h * MLA_NOPE:(h + 1) * MLA_NOPE].astype(BF16)
        k_ref[:, lo + LANES:lo + MLA_SLOT] = kr


def _mla_proj(lat, kr, cosw, sinw, qnorm, kvnorm, wq, wk, wv, S):
    T = lat.shape[0]
    tm = min(MLAP_TM, S)
    ns = S // tm
    H = MLA_HEADS
    full = lambda a: pl.BlockSpec(a.shape, lambda i: (0,) * a.ndim)
    return pl.pallas_call(
        _mla_proj_kernel,
        grid=(T // tm,),
        in_specs=[pl.BlockSpec((tm, lat.shape[1]), lambda i: (i, 0)),
                  pl.BlockSpec((tm, LANES), lambda i: (i, 0)),
                  pl.BlockSpec((tm, LANES), lambda i: (i % ns, 0)),
                  pl.BlockSpec((tm, LANES), lambda i: (i % ns, 0)),
                  full(qnorm), full(kvnorm), full(wq), full(wk), full(wv)],
        out_specs=[pl.BlockSpec((tm, H * MLA_SLOT), lambda i: (i, 0)),
                   pl.BlockSpec((tm, H * MLA_SLOT), lambda i: (i, 0)),
                   pl.BlockSpec((tm, H * MLA_V), lambda i: (i, 0))],
        out_shape=[SDS((T, H * MLA_SLOT), BF16), SDS((T, H * MLA_SLOT), BF16),
                   SDS((T, H * MLA_V), BF16)],
        compiler_params=_params(1),
        name="mla_proj",
    )(lat, kr, cosw, sinw, qnorm, kvnorm, wq, wk, wv)


def _transpose_chunks(src_ref, dst_ref):
    n, _, tk = dst_ref.shape
    for c in range(n):
        dst_ref[c] = src_ref[c * tk:(c + 1) * tk, :].astype(F32).T.astype(dst_ref.dtype)


def _online_softmax_t(carry, s, vt):
    return _online_softmax_heads((carry,), (s,), (vt,))[0]


def _online_softmax_heads(carries, ss, vts):
    stats = []
    for (m, l, acc), s in zip(carries, ss):
        m_new = jnp.maximum(m, jnp.max(s, axis=0, keepdims=True))
        alpha = jnp.exp2(m - m_new)
        p = jnp.exp2(s - m_new)
        stats.append((m_new, alpha * l + jnp.sum(p, axis=0, keepdims=True), alpha, p.astype(BF16)))
    return tuple((m_new, l, alpha * acc + _dot(vt, p))
                 for (m_new, l, alpha, p), (_, _, acc), vt in zip(stats, carries, vts))


def _mla_attn_kernel(q_ref, k_ref, v_ref, o_ref, vt_ref, *, tq, tk, scale, heads):
    qi = pl.program_id(2)
    slot, dv = MLA_SLOT, MLA_V

    @pl.when(qi == 0)
    def _():
        for g in range(heads):
            _transpose_chunks(v_ref.at[:, g * dv:(g + 1) * dv], vt_ref.at[g])

    qs = [q_ref[:, g * slot:(g + 1) * slot] for g in range(heads)]
    per_q = tq // tk

    def chunk(g, c):
        k = k_ref[pl.ds(pl.multiple_of(c * tk, tk), tk), g * slot:(g + 1) * slot]
        return lax.dot_general(k, qs[g], _NT, preferred_element_type=F32) * (scale * LOG2E), vt_ref[g, c]

    carry = [None] * heads
    for c in range(per_q):
        for g in range(heads):
            s, vt = chunk(g, qi * per_q + c)
            key = lax.broadcasted_iota(I32, s.shape, 0) + c * tk
            qry = lax.broadcasted_iota(I32, s.shape, 1)
            s = jnp.where(key <= qry, s, NEG_INF)
            if carry[g] is None:
                m = jnp.max(s, axis=0, keepdims=True)
                p = jnp.exp2(s - m)
                carry[g] = (m, jnp.sum(p, axis=0, keepdims=True), _dot(vt, p.astype(BF16)))
            else:
                carry[g] = _online_softmax_t(carry[g], s, vt)

    def body(c, carry):
        ss, vts = zip(*[chunk(g, c) for g in range(heads)])
        return _online_softmax_heads(carry, ss, vts)

    carry = lax.fori_loop(0, qi * per_q, body, tuple(carry))
    for g in range(heads):
        m, l, acc = carry[g]
        o_ref[:, g * dv:(g + 1) * dv] = (acc / l).T.astype(BF16)


def _mla_attn(q, k, v, B, S):
    H = MLA_HEADS
    G = MLA_HEADS_PER_STEP
    T = B * S
    tq = min(MLA_TQ, S)
    tk = min(MLA_TK, tq)
    nq = S // tq
    scale = (MLA_NOPE + MLA_ROPE) ** -0.5
    return pl.pallas_call(
        functools.partial(_mla_attn_kernel, tq=tq, tk=tk, scale=scale, heads=G),
        grid=(B, H // G, nq),
        in_specs=[pl.BlockSpec((tq, G * MLA_SLOT), lambda b, h, i: (b * nq + i, h)),
                  pl.BlockSpec((S, G * MLA_SLOT), lambda b, h, i: (b, h)),
                  pl.BlockSpec((S, G * MLA_V), lambda b, h, i: (b, h))],
        out_specs=pl.BlockSpec((tq, G * MLA_V), lambda b, h, i: (b * nq + i, h)),
        out_shape=SDS((T, H * MLA_V), BF16),
        scratch_shapes=[pltpu.VMEM((G, S // tk, MLA_V, tk), BF16)],
        compiler_params=_params(3),
        name="mla_attn",
    )(q, k, v)


def _t5_bucket(dist):
    n = jnp.maximum(dist, 0)
    max_exact = REL_BUCKETS // 2
    large = max_exact + (jnp.log(jnp.maximum(n, 1).astype(F32) / max_exact)
                         / math.log(REL_MAX_DIST / max_exact)
                         * (REL_BUCKETS - max_exact)).astype(I32)
    large = jnp.minimum(large, REL_BUCKETS - 1)
    return jnp.where(n < max_exact, n, large)


def _moba_bias_kernel(rb_ref, bidx_ref, o_ref):
    h = pl.program_id(0)
    for t in range(2):
        bi = bidx_ref[t]
        val = jnp.zeros(bi.shape, F32)
        for b in range(REL_BUCKETS):
            val = jnp.where(bi == b, rb_ref[b, h], val)
        o_ref[t] = val * LOG2E


def _moba_bias(rel_bias, bidx):
    H = rel_bias.shape[1]
    blk = MOBA_BLOCK
    return pl.pallas_call(
        _moba_bias_kernel,
        grid=(H,),
        in_specs=[pl.BlockSpec(memory_space=pltpu.SMEM),
                  pl.BlockSpec((2, blk, blk), lambda h: (0, 0, 0))],
        out_specs=pl.BlockSpec((None, 2, blk, blk), lambda h: (h, 0, 0, 0)),
        out_shape=SDS((H, 2, blk, blk), F32),
        compiler_params=_params(1),
        name="moba_bias",
    )(rel_bias, bidx)


def _moba_attn_kernel(rb_ref, q_ref, k_ref, v_ref, et_ref, bias_ref, o_ref, ka_ref, km_ref, qa_ref, vt_ref,
                      *, nb, n_sel, scale, heads):
    hg = pl.program_id(1)
    i = pl.program_id(2)
    blk = MOBA_BLOCK
    d = MOBA_HEAD_DIM
    nbp = -(-nb // 8) * 8

    @pl.when(i == 0)
    def _():
        for g in range(heads):
            cols = slice(g * d, (g + 1) * d)
            ka_ref[g, :, :d] = k_ref[:, cols]
            ka_ref[g, :, d:] = et_ref[...]
            _transpose_chunks(v_ref.at[:, cols], vt_ref.at[g])
            km_ref[g] = jnp.zeros(km_ref.shape[1:], F32)
            for n in range(nb):
                kb = k_ref[n * blk:(n + 1) * blk, cols].astype(F32)
                km_ref[g, n:n + 1, :] = jnp.sum(kb, axis=0, keepdims=True) * (1.0 / blk)

    qas = []
    for g in range(heads):
        q = q_ref[:, g * d:(g + 1) * d]
        gate = lax.dot_general(km_ref[g, :nbp, :].astype(BF16), q, _NT, preferred_element_type=F32)
        sub = lax.broadcasted_iota(I32, gate.shape, 0)
        gt = jnp.where(sub < i, gate, NEG_INF)
        keep = jnp.full(gate.shape, NEG_INF, F32)
        for _ in range(n_sel):
            mx = jnp.max(gt, axis=0, keepdims=True)
            first = jnp.min(jnp.where(gt == mx, sub, LANES), axis=0, keepdims=True)
            pick = sub == first
            keep = jnp.where(pick & (sub < i), 0.0, keep)
            gt = jnp.where(pick, -3.0e38, gt)
        keep = jnp.where(sub == i, 0.0, keep)
        keep = jnp.concatenate([keep, jnp.full((LANES - nbp, blk), NEG_INF, F32)], axis=0)
        qa_ref[g, :, :d] = q
        qa_ref[g, :, d:] = keep.T.astype(BF16)
        qas.append(qa_ref[g])

    def block(g, n):
        kk = ka_ref[g, pl.ds(pl.multiple_of(n * blk, blk), blk), :]
        return lax.dot_general(kk, qas[g], _NT, preferred_element_type=F32) * (scale * LOG2E), vt_ref[g, n]

    carry = []
    for g in range(heads):
        s, vt = block(g, i)
        s = s + bias_ref[g, 0]
        key = lax.broadcasted_iota(I32, s.shape, 0)
        qry = lax.broadcasted_iota(I32, s.shape, 1)
        s = jnp.where(key <= qry, s, NEG_INF)
        m = jnp.max(s, axis=0, keepdims=True)
        p = jnp.exp2(s - m)
        carry.append((m, jnp.sum(p, axis=0, keepdims=True), _dot(vt, p.astype(BF16))))

    def adjacent(n, carry):
        ss, vts = zip(*[block(g, n) for g in range(heads)])
        ss = [s + bias_ref[g, 1] for g, s in enumerate(ss)]
        return _online_softmax_heads(carry, ss, vts)

    far_bias = [rb_ref[REL_BUCKETS - 1, hg * heads + g] * LOG2E for g in range(heads)]

    def far(n, carry):
        ss, vts = zip(*[block(g, n) for g in range(heads)])
        ss = [s + far_bias[g] for g, s in enumerate(ss)]
        return _online_softmax_heads(carry, ss, vts)

    n_far = jnp.maximum(i - 1, 0)
    carry = lax.fori_loop(n_far, i, adjacent, tuple(carry))
    carry = lax.fori_loop(0, n_far, far, carry)
    for g in range(heads):
        m, l, acc = carry[g]
        o_ref[:, g * d:(g + 1) * d] = (acc / l).T.astype(BF16)


def _moba_attn(qkv, et, bias, rel_bias, B, S):
    H, d, blk = MOBA_HEADS, MOBA_HEAD_DIM, MOBA_BLOCK
    T = B * S
    nb = S // blk
    n_sel = min(MOBA_TOPK, nb)
    G = MOBA_HEADS_PER_STEP
    ng = H // G
    return pl.pallas_call(
        functools.partial(_moba_attn_kernel, nb=nb, n_sel=n_sel, scale=d ** -0.5, heads=G),
        grid=(B, ng, nb),
        in_specs=[pl.BlockSpec(memory_space=pltpu.SMEM),
                  pl.BlockSpec((blk, G * d), lambda b, h, i: (b * nb + i, h)),
                  pl.BlockSpec((S, G * d), lambda b, h, i: (b, ng + h)),
                  pl.BlockSpec((S, G * d), lambda b, h, i: (b, 2 * ng + h)),
                  pl.BlockSpec((S, LANES), lambda b, h, i: (0, 0)),
                  pl.BlockSpec((G, 2, blk, blk), lambda b, h, i: (h, 0, 0, 0))],
        out_specs=pl.BlockSpec((blk, G * d), lambda b, h, i: (b * nb + i, h)),
        out_shape=SDS((T, H * d), BF16),
        scratch_shapes=[pltpu.VMEM((G, S, d + LANES), BF16),
                        pltpu.VMEM((G, LANES, d), F32),
                        pltpu.VMEM((G, blk, d + LANES), BF16),
                        pltpu.VMEM((G, nb, d, blk), BF16)],
        compiler_params=_params(3),
        name="moba_attn",
    )(rel_bias, qkv, qkv, qkv, et, bias)


def _out_proj_kernel(ya_ref, yb_ref, g_ref, x_ref, woa_ref, wob_ref, wout_ref, mg_ref, wr_ref, br_ref,
                     h1_ref, xn_ref, lg_ref, *, D):
    a = _dot(ya_ref[...], woa_ref[...])
    b = _dot(yb_ref[...], wob_ref[...])
    g = g_ref[...]
    merged = jax.nn.sigmoid(g[:, :D]) * a + jax.nn.sigmoid(g[:, D:]) * b
    h1 = x_ref[...] + _dot(merged.astype(BF16), wout_ref[...])
    h1_ref[...] = h1
    xn = _rms(h1, mg_ref[...]).astype(BF16)
    lg_ref[...] = _dot(xn, wr_ref[...]) + br_ref[...]
    xn_ref[...] = _pack_bf16_pairs(xn)


def _pack_bf16_pairs(x):
    bits = lax.bitcast_convert_type(x.astype(F32), U32)
    half = x.shape[1] // 2
    return (bits[:, :half] >> 16) | (bits[:, half:] & U32(HIGH_HALF))


def _unpack_bf16_pairs(w):
    lo = lax.bitcast_convert_type(w << 16, F32).astype(BF16)
    hi = lax.bitcast_convert_type(w & U32(HIGH_HALF), F32).astype(BF16)
    return lo, hi


def _resident(a):
    return pl.BlockSpec(a.shape, lambda i: (0,) * a.ndim, pipeline_mode=pl.Buffered(1))


def _out_proj(ya, yb, gates, x2, woa, wob, wout, mgain, wr, br):
    T, D = x2.shape
    tm = min(OUT_TM, T)
    row = lambda a: pl.BlockSpec((tm, a.shape[1]), lambda i: (i, 0))
    return pl.pallas_call(
        functools.partial(_out_proj_kernel, D=D),
        grid=(T // tm,),
        in_specs=[row(ya), row(yb), row(gates), row(x2),
                  _resident(woa), _resident(wob), _resident(wout), _resident(mgain),
                  _resident(wr), _resident(br)],
        out_specs=[pl.BlockSpec((tm, D), lambda i: (i, 0)),
                   pl.BlockSpec((tm, D // 2), lambda i: (i, 0)),
                   pl.BlockSpec((tm, LANES), lambda i: (i, 0))],
        out_shape=[SDS((T, D), F32), SDS((T, D // 2), U32), SDS((T, LANES), F32)],
        compiler_params=_params(1),
        name="out_proj",
    )(ya, yb, gates, x2, woa, wob, wout, mgain, wr, br)


def _route_kernel(lg_ref, e_ref, w_ref, r_ref, cnt_ref, carry_ref, *, n_exp):
    i = pl.program_id(0)

    @pl.when(i == 0)
    def _():
        carry_ref[...] = jnp.zeros(carry_ref.shape, F32)

    lt = lg_ref[...].T
    tm = lt.shape[1]
    sub = lax.broadcasted_iota(I32, lt.shape, 0)
    neg = -jnp.inf
    cur = jnp.where(sub < n_exp, lt, neg)
    vals, idxs = [], []
    for _ in range(TOP_K):
        mx = jnp.max(cur, axis=0, keepdims=True)
        ix = jnp.min(jnp.where(cur == mx, sub, LANES), axis=0, keepdims=True)
        vals.append(mx)
        idxs.append(ix)
        cur = jnp.where(sub == ix, neg, cur)
    ex = [jnp.exp(v - vals[0]) for v in vals]
    den = ex[0]
    for e in ex[1:]:
        den = den + e
    onehot = jnp.zeros(lt.shape, F32)
    for ix in idxs:
        onehot = onehot + (sub == ix).astype(F32)
    r_i = lax.broadcasted_iota(I32, (tm, tm), 0)
    c_i = lax.broadcasted_iota(I32, (tm, tm), 1)
    tri = (r_i <= c_i).astype(BF16)
    incl = _dot(onehot.astype(BF16), tri)
    base = carry_ref[...]
    excl = incl - onehot + base[:, :1]
    e_ref[...] = jnp.zeros(e_ref.shape, I32)
    w_ref[...] = jnp.zeros(w_ref.shape, F32)
    r_ref[...] = jnp.zeros(r_ref.shape, I32)
    for k in range(TOP_K):
        e_ref[k:k + 1, :] = idxs[k]
        w_ref[k:k + 1, :] = ex[k] / den
        rk = jnp.sum(jnp.where(sub == idxs[k], excl, 0.0), axis=0, keepdims=True)
        r_ref[k:k + 1, :] = rk.astype(I32)
    total = base + jnp.sum(onehot, axis=1, keepdims=True)
    carry_ref[...] = total
    cnt_ref[...] = total


def _route(logits, n_exp):
    T = logits.shape[0]
    tm = min(ROUTE_TM, T)
    return pl.pallas_call(
        functools.partial(_route_kernel, n_exp=n_exp),
        grid=(T // tm,),
        in_specs=[pl.BlockSpec((tm, LANES), lambda i: (i, 0))],
        out_specs=[pl.BlockSpec((8, tm), lambda i: (0, i)),
                   pl.BlockSpec((8, tm), lambda i: (0, i)),
                   pl.BlockSpec((8, tm), lambda i: (0, i)),
                   pl.BlockSpec((LANES, LANES), lambda i: (0, 0))],
        out_shape=[SDS((8, T), I32), SDS((8, T), F32), SDS((8, T), I32), SDS((LANES, LANES), F32)],
        scratch_shapes=[pltpu.VMEM((LANES, LANES), F32)],
        compiler_params=_params(1),
        name="route",
    )(logits)


def _dest_kernel(e_ref, r_ref, base_ref, d_ref):
    base = base_ref[...][:, :1]
    sub = lax.broadcasted_iota(I32, (LANES, e_ref.shape[1]), 0)
    d_ref[...] = jnp.zeros(d_ref.shape, I32)
    for k in range(TOP_K):
        off = jnp.sum(jnp.where(sub == e_ref[k:k + 1, :], base, 0), axis=0, keepdims=True)
        d_ref[k:k + 1, :] = r_ref[k:k + 1, :] + off


def _dest(e_k, r_k, base):
    T = e_k.shape[1]
    tb = min(DEST_TB, T)
    blk = pl.BlockSpec((8, tb), lambda i: (0, i))
    return pl.pallas_call(
        _dest_kernel,
        grid=(T // tb,),
        in_specs=[blk, blk, pl.BlockSpec((LANES, LANES), lambda i: (0, 0))],
        out_specs=blk,
        out_shape=SDS((8, T), I32),
        compiler_params=_params(1),
        name="dest",
    )(e_k, r_k, base)


def _dispatch_kernel(zflag_ref, dest_ref, x_ref, xg_hbm, zbuf_ref, sem, *, tm, rows, n_tiles):
    i = pl.program_id(0)

    @pl.when(i == 0)
    def _():
        zbuf_ref[...] = jnp.zeros(zbuf_ref.shape, zbuf_ref.dtype)

        def zcopy(j):
            start = pl.multiple_of(j * rows, rows)
            return pltpu.make_async_copy(zbuf_ref, xg_hbm.at[pl.ds(start, rows)], sem)

        def zstart(j, c):
            @pl.when(zflag_ref[j] != 0)
            def _():
                zcopy(j).start()
            return c

        def zwait(j, c):
            @pl.when(zflag_ref[j] != 0)
            def _():
                zcopy(j).wait()
            return c

        lax.fori_loop(0, n_tiles, zstart, 0)
        lax.fori_loop(0, n_tiles, zwait, 0)

    def copy(t, k):
        return pltpu.make_async_copy(x_ref.at[pl.ds(t, 1)], xg_hbm.at[pl.ds(dest_ref[k, t], 1)], sem)

    def start(t, c):
        for k in range(TOP_K):
            copy(t, k).start()
        return c

    def wait(t, c):
        for k in range(TOP_K):
            copy(t, k).wait()
        return c

    lax.fori_loop(0, tm, start, 0, unroll=4)
    lax.fori_loop(0, tm, wait, 0, unroll=4)


def _dispatch(zflag, dest, xn):
    T, D = xn.shape
    tm = min(DISPATCH_TM, T)
    n_chunks = zflag.shape[0]
    return pl.pallas_call(
        functools.partial(_dispatch_kernel, tm=tm, rows=ZERO_ROWS, n_tiles=n_chunks),
        grid_spec=pltpu.PrefetchScalarGridSpec(
            num_scalar_prefetch=1,
            grid=(T // tm,),
            in_specs=[pl.BlockSpec((8, tm), lambda i, zf: (0, i), memory_space=pltpu.SMEM),
                      pl.BlockSpec((tm, D), lambda i, zf: (i, 0))],
            out_specs=pl.BlockSpec(memory_space=pl.ANY),
            scratch_shapes=[pltpu.VMEM((ZERO_ROWS, D), xn.dtype), pltpu.SemaphoreType.DMA]),
        out_shape=SDS((n_chunks * ZERO_ROWS, D), xn.dtype),
        compiler_params=_params(1),
        name="dispatch",
    )(zflag, dest, xn)


def _row_blocks(valid, n_rows, compute, out_ref):
    full = valid > n_rows - EXPERT_SUB

    @pl.when(full)
    def _():
        compute(slice(0, n_rows))

    for sb in range(n_rows // EXPERT_SUB):
        rows = slice(sb * EXPERT_SUB, (sb + 1) * EXPERT_SUB)

        @pl.when(jnp.logical_not(full) & (sb * EXPERT_SUB < valid))
        def _():
            compute(rows)

        @pl.when(jnp.logical_not(full) & (sb * EXPERT_SUB >= valid))
        def _():
            out_ref[rows, :] = jnp.zeros((EXPERT_SUB, out_ref.shape[1]), out_ref.dtype)


def _ffn_up_kernel(te_ref, tv_ref, nu_ref, x_ref, wg_ref, wu_ref, bg_ref, bu_ref, h_ref, wgb_ref, wub_ref):
    i = pl.program_id(1)
    valid = tv_ref[i]
    new_expert = (i == 0) | (te_ref[i] != te_ref[jnp.maximum(i - 1, 0)])

    @pl.when((valid > 0) & new_expert)
    def _():
        wgb_ref[...] = wg_ref[...].astype(BF16)
        wub_ref[...] = wu_ref[...].astype(BF16)

    half = x_ref.shape[1]

    def compute(rows):
        lo, hi = _unpack_bf16_pairs(x_ref[rows, :])
        g = _dot(lo, wgb_ref[:half, :]) + _dot(hi, wgb_ref[half:, :]) + bg_ref[...]
        u = _dot(lo, wub_ref[:half, :]) + _dot(hi, wub_ref[half:, :]) + bu_ref[...]
        g = jnp.minimum(g, SWIGLU_LIMIT)
        u = jnp.clip(u, -SWIGLU_LIMIT, SWIGLU_LIMIT)
        h_ref[rows, :] = (g * jax.nn.sigmoid(SWIGLU_ALPHA * g) * (u + 1.0)).astype(BF16)

    _row_blocks(valid, x_ref.shape[0], compute, h_ref)


def _ffn_up(tile_expert, tile_valid, n_used, xg, wg, wu, bg, bu):
    P = xg.shape[0]
    D, Dx = wg.shape[1], wg.shape[2]
    tr = EXPERT_ROWS
    tn = min(EXPERT_TN_UP, Dx)
    n_tiles = P // tr
    xmap = lambda j, i, te, tv, nu: (jnp.minimum(i, nu[0] - 1), 0)
    wmap = lambda j, i, te, tv, nu: (te[i], 0, j)
    return pl.pallas_call(
        _ffn_up_kernel,
        grid_spec=pltpu.PrefetchScalarGridSpec(
            num_scalar_prefetch=3,
            grid=(Dx // tn, n_tiles),
            in_specs=[pl.BlockSpec((tr, D // 2), xmap),
                      pl.BlockSpec((None, D, tn), wmap),
                      pl.BlockSpec((None, D, tn), wmap),
                      pl.BlockSpec((None, 1, tn), wmap),
                      pl.BlockSpec((None, 1, tn), wmap)],
            out_specs=pl.BlockSpec((tr, tn), lambda j, i, te, tv, nu: (i, j)),
            scratch_shapes=[pltpu.VMEM((D, tn), BF16), pltpu.VMEM((D, tn), BF16)]),
        out_shape=SDS((P, Dx), BF16),
        compiler_params=_params(2),
        name="ffn_up",
    )(tile_expert, tile_valid, n_used, xg, wg, wu, bg, bu)


def _ffn_down_kernel(te_ref, tv_ref, nu_ref, h_ref, wd_ref, bd_ref, y_ref, wdb_ref):
    i = pl.program_id(1)
    valid = tv_ref[i]
    new_expert = (i == 0) | (te_ref[i] != te_ref[jnp.maximum(i - 1, 0)])

    @pl.when((valid > 0) & new_expert)
    def _():
        wdb_ref[...] = wd_ref[...].astype(BF16)

    def compute(rows):
        y = _dot(h_ref[rows, :], wdb_ref[...]) + bd_ref[...]
        y_ref[rows, :] = _pack_bf16_pairs(y.astype(BF16))

    _row_blocks(valid, h_ref.shape[0], compute, y_ref)


def _ffn_down(tile_expert, tile_valid, n_used, hid, wd, bd):
    P, Dx = hid.shape
    D = wd.shape[2]
    tr = EXPERT_ROWS
    tn = min(EXPERT_TN_DOWN, D)
    n_tiles = P // tr
    hmap = lambda j, i, te, tv, nu: (jnp.minimum(i, nu[0] - 1), 0)
    wmap = lambda j, i, te, tv, nu: (te[i], 0, j)
    return pl.pallas_call(
        _ffn_down_kernel,
        grid_spec=pltpu.PrefetchScalarGridSpec(
            num_scalar_prefetch=3,
            grid=(D // tn, n_tiles),
            in_specs=[pl.BlockSpec((tr, Dx), hmap),
                      pl.BlockSpec((None, Dx, tn), wmap),
                      pl.BlockSpec((None, 1, tn), wmap)],
            out_specs=pl.BlockSpec((tr, tn // 2), lambda j, i, te, tv, nu: (i, j)),
            scratch_shapes=[pltpu.VMEM((Dx, tn), BF16)]),
        out_shape=SDS((P, D // 2), U32),
        compiler_params=_params(2),
        name="ffn_down",
    )(tile_expert, tile_valid, n_used, hid, wd, bd)


def _combine_kernel(dest_ref, w_ref, h1_ref, p_ref, wple_ref, wpg_ref, pg_ref, fg_ref, y_hbm,
                    o_ref, ybuf_ref, sem, *, tm, pack, final):
    def copy(t, k):
        return pltpu.make_async_copy(y_hbm.at[pl.ds(dest_ref[k, t], 1)],
                                     ybuf_ref.at[k, pl.ds(t, 1)], sem)

    def start(t, c):
        for k in range(TOP_K):
            copy(t, k).start()
        return c

    def wait(t, c):
        for k in range(TOP_K):
            copy(t, k).wait()
        return c

    lax.fori_loop(0, tm, start, 0, unroll=4)
    pw = _dot(p_ref[...].astype(BF16), wple_ref[...])
    lax.fori_loop(0, tm, wait, 0, unroll=4)
    w = w_ref[...]
    parts = []
    for c in range(ybuf_ref.shape[2] // pack):
        cols = slice(c * pack, (c + 1) * pack)
        lo = hi = None
        for k in range(TOP_K):
            words = ybuf_ref[k, :, cols]
            lo_k = w[:, k:k + 1] * lax.bitcast_convert_type(words << 16, F32)
            hi_k = w[:, k:k + 1] * lax.bitcast_convert_type(words & U32(HIGH_HALF), F32)
            lo = lo_k if lo is None else lo + lo_k
            hi = hi_k if hi is None else hi + hi_k
        parts += [lo, hi]
    h2 = h1_ref[...] + jnp.concatenate(parts, axis=1)
    xn = _rms(h2, pg_ref[...]).astype(BF16)
    gate = jax.nn.sigmoid(_dot(xn, wpg_ref[...]))
    h3 = h2 + pw * gate
    o_ref[...] = _rms(h3, fg_ref[...]) if final else h3


def _combine(dest, wts, h1, p2, wple, wpg, pgain, fgain, yg, final):
    T, D = h1.shape
    tm = min(COMBINE_TM, T)
    row = lambda a: pl.BlockSpec((tm, a.shape[1]), lambda i: (i, 0))
    return pl.pallas_call(
        functools.partial(_combine_kernel, tm=tm, pack=min(EXPERT_TN_DOWN, D) // 2, final=final),
        grid=(T // tm,),
        in_specs=[pl.BlockSpec((8, tm), lambda i: (0, i), memory_space=pltpu.SMEM),
                  row(wts), row(h1), row(p2),
                  _resident(wple), _resident(wpg), _resident(pgain), _resident(fgain),
                  pl.BlockSpec(memory_space=pl.ANY)],
        out_specs=pl.BlockSpec((tm, D), lambda i: (i, 0)),
        out_shape=SDS((T, D), F32),
        scratch_shapes=[pltpu.VMEM((TOP_K, tm, D // 2), U32), pltpu.SemaphoreType.DMA],
        compiler_params=_params(1),
        name="combine",
    )(dest, wts, h1, p2, wple, wpg, pgain, fgain, yg)


def kernel(x, p, attn_norm, w_in, q_lat_norm, kv_lat_norm, w_uq, w_ukv, w_o_mla, w_o_moba, w_out,
           rel_bias, moe_norm, w_router, b_router, w_gate, b_gate, w_up, b_up, w_down, b_down,
           ple_norm, w_ple_gate, w_ple, final_norm):
    B, S, D = x.shape
    T = B * S
    n_layers = w_in.shape[0]
    E = w_router.shape[-1]
    H = MLA_HEADS
    mw = MOBA_HEADS * MOBA_HEAD_DIM
    assert S % MOBA_BLOCK == 0 and E <= LANES
    o_kr = MLA_Q_LORA + MLA_KV_LORA
    o_q = o_kr + MLA_ROPE
    o_g = o_q + 3 * mw

    inv = 1.0 / (ROPE_THETA ** (jnp.arange(0, MLA_ROPE, 2, dtype=F32) / MLA_ROPE))
    ang = jnp.arange(S, dtype=F32)[:, None] * inv[None, :]
    cos, sin = jnp.cos(ang), jnp.sin(ang)
    zpad = jnp.zeros((S, LANES - MLA_ROPE), F32)
    cosw = jnp.concatenate([cos, cos, zpad], axis=1)
    sinw = jnp.concatenate([-sin, sin, zpad], axis=1)
    r = jnp.arange(MOBA_BLOCK)
    d0 = r[None, :] - r[:, None]
    bidx = jnp.stack([_t5_bucket(d0), _t5_bucket(d0 + MOBA_BLOCK)]).astype(I32)
    et = (jnp.arange(S)[:, None] // MOBA_BLOCK == jnp.arange(LANES)[None, :]).astype(BF16)
    bias = _moba_bias(rel_bias, bidx)

    h = x.reshape(T, D)
    for li in range(n_layers):
        w = w_in[li]
        w_main = jnp.concatenate([w[:, :o_kr], w[:, o_q:]], axis=1).astype(BF16)
        w_kr = jnp.pad(w[:, o_kr:o_q], ((0, 0), (0, LANES - MLA_ROPE))).astype(BF16)
        wq = w_uq[li].reshape(MLA_Q_LORA, H, MLA_NOPE + MLA_ROPE)
        wq = jnp.pad(wq, ((0, 0), (0, 0), (0, MLA_SLOT - MLA_NOPE - MLA_ROPE)))
        wq = wq.reshape(MLA_Q_LORA, H * MLA_SLOT).astype(BF16)
        wkv = w_ukv[li].reshape(MLA_KV_LORA, H, MLA_NOPE + MLA_V)
        wk = wkv[:, :, :MLA_NOPE].reshape(MLA_KV_LORA, H * MLA_NOPE).astype(BF16)
        wv = wkv[:, :, MLA_NOPE:].reshape(MLA_KV_LORA, H * MLA_V).astype(BF16)
        wr = jnp.pad(w_router[li], ((0, 0), (0, LANES - E))).astype(BF16)
        br = jnp.pad(b_router[li], (0, LANES - E), constant_values=NEG_INF)[None, :]

        lat, qkv, gates, kr = _in_proj(h, attn_norm[li][None, :], w_main, w_kr, o_kr, 3 * mw)
        q_a, k_a, v_a = _mla_proj(lat, kr, cosw, sinw, q_lat_norm[li][None, :], kv_lat_norm[li][None, :],
                                  wq, wk, wv, S)
        y_a = _mla_attn(q_a, k_a, v_a, B, S)
        y_b = _moba_attn(qkv, et, bias, rel_bias, B, S)
        h1, xn, logits = _out_proj(y_a, y_b, gates, h, w_o_mla[li].astype(BF16), w_o_moba[li].astype(BF16),
                                   w_out[li].astype(BF16), moe_norm[li][None, :], wr, br)

        e_k, w_k, r_k, cnt = _route(logits, E)
        counts = cnt[:E, 0].astype(I32)
        tiles = (counts + EXPERT_ROWS - 1) // EXPERT_ROWS
        tile_end = jnp.cumsum(tiles)
        tile_start = tile_end - tiles
        n_tiles = (T * TOP_K) // EXPERT_ROWS + E
        n_used = tile_end[-1]
        tile_ids = jnp.arange(n_tiles)
        capped = jnp.minimum(tile_ids, n_used - 1)
        tile_expert = jnp.minimum(jnp.sum(tile_end[None, :] <= capped[:, None], axis=1), E - 1).astype(I32)
        base = jnp.pad(tile_start * EXPERT_ROWS, (0, LANES - E)).astype(I32)
        dest = _dest(e_k, r_k, jnp.broadcast_to(base[:, None], (LANES, LANES)))
        mine = tile_expert[:, None] == jnp.arange(E)[None, :]
        in_tile = jnp.sum(jnp.where(mine, counts[None, :] - (tile_ids[:, None] - tile_start[None, :]) * EXPERT_ROWS, 0),
                          axis=1)
        tile_valid = jnp.where(tile_ids < n_used, jnp.clip(in_tile, 0, EXPERT_ROWS), 0).astype(I32)
        chunk_lo = jnp.arange(n_tiles * EXPERT_ROWS // ZERO_ROWS) * ZERO_ROWS
        pad_lo = tile_start * EXPERT_ROWS + counts
        pad_hi = tile_end * EXPERT_ROWS
        in_pad = (chunk_lo[:, None] < pad_hi[None, :]) & (chunk_lo[:, None] + ZERO_ROWS > pad_lo[None, :])
        zflag = (jnp.any(in_pad, axis=1) | (chunk_lo >= n_used * EXPERT_ROWS)).astype(I32)
        nu = n_used.astype(I32)[None]

        xg = _dispatch(zflag, dest, xn)
        hid = _ffn_up(tile_expert, tile_valid, nu, xg, w_gate[li], w_up[li],
                      b_gate[li][:, None, :], b_up[li][:, None, :])
        yg = _ffn_down(tile_expert, tile_valid, nu, hid, w_down[li], b_down[li][:, None, :])
        h = _combine(dest, w_k[:TOP_K].T, h1, p[li].reshape(T, -1), w_ple[li].astype(BF16),
                     w_ple_gate[li].astype(BF16), ple_norm[li][None, :], final_norm[None, :], yg,
                     final=li == n_layers - 1)
    return h.reshape(B, S, D)
```

```python
import functools
import math

import jax
import jax.numpy as jnp
from jax import lax
from jax.experimental import pallas as pl
from jax.experimental.pallas import tpu as pltpu

F32 = jnp.float32
BF16 = jnp.bfloat16
I32 = jnp.int32
U32 = jnp.uint32
HIGH_HALF = 0xFFFF0000
SDS = jax.ShapeDtypeStruct

EPS = 1e-6
NEG_INF = -1e30
LOG2E = math.log2(math.e)
MLA_HEADS = 8
MLA_NOPE = 128
MLA_ROPE = 64
MLA_V = 128
MLA_Q_LORA = 512
MLA_KV_LORA = 512
ROPE_THETA = 10000.0
MOBA_HEADS = 8
MOBA_HEAD_DIM = 128
MOBA_BLOCK = 256
MOBA_TOPK = 3
REL_BUCKETS = 32
REL_MAX_DIST = 128
TOP_K = 4
SWIGLU_LIMIT = 7.0
SWIGLU_ALPHA = 1.702

LANES = 128
MLA_SLOT = 2 * LANES
VMEM_LIMIT = 56 * 2**20

IN_TM, IN_TN = 512, 1024
MLAP_TM = 512
MLA_TQ = 512
MLA_TK = 512
MLA_HEADS_PER_STEP = 4
MOBA_HEADS_PER_STEP = 4
OUT_TM = 256
ROUTE_TM = 512
EXPERT_ROWS = 1024
EXPERT_SUB = 256
EXPERT_TN_UP = 512
EXPERT_TN_DOWN = 1024
ZERO_ROWS = 512
DISPATCH_TM = 512
DEST_TB = 2048
COMBINE_TM = 512

_NT = (((1,), (1,)), ((), ()))


def _params(n_axes):
    return pltpu.CompilerParams(dimension_semantics=("arbitrary",) * n_axes,
                                vmem_limit_bytes=VMEM_LIMIT)


def _rms(x, g):
    return x * lax.rsqrt(jnp.mean(x * x, axis=-1, keepdims=True) + EPS) * g


def _dot(a, b):
    return jnp.dot(a, b, preferred_element_type=F32)


def _in_proj_kernel(x_ref, g_ref, w_ref, wkr_ref, lat_ref, qkv_ref, gates_ref, kr_ref, xn_ref,
                    *, n_lat, n_qkv):
    j = pl.program_id(1)

    @pl.when(j == 0)
    def _():
        xn = _rms(x_ref[...], g_ref[...]).astype(BF16)
        xn_ref[...] = xn
        kr_ref[...] = _dot(xn, wkr_ref[...])

    @pl.when(j < n_lat)
    def _():
        lat_ref[...] = _dot(xn_ref[...], w_ref[...])

    @pl.when((j >= n_lat) & (j < n_lat + n_qkv))
    def _():
        qkv_ref[...] = _dot(xn_ref[...], w_ref[...]).astype(BF16)

    @pl.when(j >= n_lat + n_qkv)
    def _():
        gates_ref[...] = _dot(xn_ref[...], w_ref[...])


def _in_proj(x2, gain, w_main, w_kr, n_lat_cols, n_qkv_cols):
    T, D = x2.shape
    n_g_cols = w_main.shape[1] - n_lat_cols - n_qkv_cols
    tm = min(IN_TM, T)
    tn = math.gcd(math.gcd(IN_TN, n_lat_cols), math.gcd(n_qkv_cols, n_g_cols))
    n_lat, n_qkv, n_g = n_lat_cols // tn, n_qkv_cols // tn, n_g_cols // tn
    return pl.pallas_call(
        functools.partial(_in_proj_kernel, n_lat=n_lat, n_qkv=n_qkv),
        grid=(T // tm, n_lat + n_qkv + n_g),
        in_specs=[pl.BlockSpec((tm, D), lambda i, j: (i, 0)),
                  pl.BlockSpec((1, D), lambda i, j: (0, 0)),
                  pl.BlockSpec((D, tn), lambda i, j: (0, j)),
                  pl.BlockSpec((D, LANES), lambda i, j: (0, 0))],
        out_specs=[pl.BlockSpec((tm, tn), lambda i, j: (i, jnp.minimum(j, n_lat - 1))),
                   pl.BlockSpec((tm, tn), lambda i, j: (i, jnp.clip(j - n_lat, 0, n_qkv - 1))),
                   pl.BlockSpec((tm, tn), lambda i, j: (i, jnp.maximum(j - n_lat - n_qkv, 0))),
                   pl.BlockSpec((tm, LANES), lambda i, j: (i, 0))],
        out_shape=[SDS((T, n_lat_cols), F32), SDS((T, n_qkv_cols), BF16),
                   SDS((T, n_g_cols), F32), SDS((T, LANES), F32)],
        scratch_shapes=[pltpu.VMEM((tm, D), BF16)],
        compiler_params=_params(2),
        name="in_proj",
    )(x2, gain, w_main, w_kr)


def _mla_proj_kernel(lat_ref, kr_ref, cos_ref, sin_ref, qn_ref, kvn_ref, wq_ref, wk_ref, wv_ref,
                     q_ref, k_ref, v_ref):
    lat = lat_ref[...]
    qn = _rms(lat[:, :MLA_Q_LORA], qn_ref[...]).astype(BF16)
    kvn = _rms(lat[:, MLA_Q_LORA:], kvn_ref[...]).astype(BF16)
    q = _dot(qn, wq_ref[...])
    kn = _dot(kvn, wk_ref[...])
    v_ref[...] = _dot(kvn, wv_ref[...]).astype(BF16)
    c = cos_ref[...]
    s = sin_ref[...]
    half = MLA_ROPE // 2
    lane = lax.broadcasted_iota(I32, c.shape, 1)

    def rope(xr):
        swapped = jnp.where(lane < half, pltpu.roll(xr, LANES - half, 1), pltpu.roll(xr, half, 1))
        return xr * c + swapped * s

    kr = rope(kr_ref[...]).astype(BF16)
    for h in range(MLA_HEADS):
        lo = h * MLA_SLOT
        q_ref[:, lo:lo + LANES] = q[:, lo:lo + LANES].astype(BF16)
        q_ref[:, lo + LANES:lo + MLA_SLOT] = rope(q[:, lo + LANES:lo + MLA_SLOT]).astype(BF16)
        k_ref[:, lo:lo + LANES] = kn[:, h * MLA_NOPE:(h + 1) * MLA_NOPE].astype(BF16)
        k_ref[:, lo + LANES:lo + MLA_SLOT] = kr


def _mla_proj(lat, kr, cosw, sinw, qnorm, kvnorm, wq, wk, wv, S):
    T = lat.shape[0]
    tm = min(MLAP_TM, S)
    ns = S // tm
    H = MLA_HEADS
    full = lambda a: pl.BlockSpec(a.shape, lambda i: (0,) * a.ndim)
    return pl.pallas_call(
        _mla_proj_kernel,
        grid=(T // tm,),
        in_specs=[pl.BlockSpec((tm, lat.shape[1]), lambda i: (i, 0)),
                  pl.BlockSpec((tm, LANES), lambda i: (i, 0)),
                  pl.BlockSpec((tm, LANES), lambda i: (i % ns, 0)),
                  pl.BlockSpec((tm, LANES), lambda i: (i % ns, 0)),
                  full(qnorm), full(kvnorm), full(wq), full(wk), full(wv)],
        out_specs=[pl.BlockSpec((tm, H * MLA_SLOT), lambda i: (i, 0)),
                   pl.BlockSpec((tm, H * MLA_SLOT), lambda i: (i, 0)),
                   pl.BlockSpec((tm, H * MLA_V), lambda i: (i, 0))],
        out_shape=[SDS((T, H * MLA_SLOT), BF16), SDS((T, H * MLA_SLOT), BF16),
                   SDS((T, H * MLA_V), BF16)],
        compiler_params=_params(1),
        name="mla_proj",
    )(lat, kr, cosw, sinw, qnorm, kvnorm, wq, wk, wv)


def _transpose_chunks(src_ref, dst_ref):
    n, _, tk = dst_ref.shape
    for c in range(n):
        dst_ref[c] = src_ref[c * tk:(c + 1) * tk, :].astype(F32).T.astype(dst_ref.dtype)


def _online_softmax_t(carry, s, vt):
    return _online_softmax_heads((carry,), (lambda: s,), (vt,))[0]


def _online_softmax_heads(carries, score_fns, vts):
    n = len(carries)
    scores, stats, out = {}, {}, [None] * n
    for step in range(n + 2):
        if step < n:
            scores[step] = score_fns[step]()
        g = step - 1
        if 0 <= g < n:
            m, l, _ = carries[g]
            s = scores.pop(g)
            m_new = jnp.maximum(m, jnp.max(s, axis=0, keepdims=True))
            alpha = jnp.exp2(m - m_new)
            p = jnp.exp2(s - m_new)
            stats[g] = (m_new, alpha * l + jnp.sum(p, axis=0, keepdims=True), alpha, p.astype(BF16))
        g = step - 2
        if 0 <= g < n:
            m_new, l, alpha, p = stats.pop(g)
            chunks = vts[g] if isinstance(vts[g], (tuple, list)) else (vts[g],)
            rows = p.shape[0] // len(chunks)
            pv = _dot(chunks[0], p[:rows])
            for c in range(1, len(chunks)):
                pv = pv + _dot(chunks[c], p[c * rows:(c + 1) * rows])
            out[g] = (m_new, l, alpha * carries[g][2] + pv)
    return tuple(out)


def _mla_attn_kernel(q_ref, k_ref, v_ref, o_ref, vt_ref, *, tq, tk, scale, heads):
    qi = pl.program_id(2)
    slot, dv = MLA_SLOT, MLA_V

    @pl.when(qi == 0)
    def _():
        for g in range(heads):
            _transpose_chunks(v_ref.at[:, g * dv:(g + 1) * dv], vt_ref.at[g])

    qs = [q_ref[:, g * slot:(g + 1) * slot] for g in range(heads)]
    per_q = tq // tk

    def chunk(g, c):
        k = k_ref[pl.ds(pl.multiple_of(c * tk, tk), tk), g * slot:(g + 1) * slot]
        return lax.dot_general(k, qs[g], _NT, preferred_element_type=F32) * (scale * LOG2E), vt_ref[g, c]

    carry = [None] * heads
    for c in range(per_q):
        for g in range(heads):
            s, vt = chunk(g, qi * per_q + c)
            key = lax.broadcasted_iota(I32, s.shape, 0) + c * tk
            qry = lax.broadcasted_iota(I32, s.shape, 1)
            s = jnp.where(key <= qry, s, NEG_INF)
            if carry[g] is None:
                m = jnp.max(s, axis=0, keepdims=True)
                p = jnp.exp2(s - m)
                carry[g] = (m, jnp.sum(p, axis=0, keepdims=True), _dot(vt, p.astype(BF16)))
            else:
                carry[g] = _online_softmax_t(carry[g], s, vt)

    def body(c, carry):
        fns = [lambda g=g: chunk(g, c)[0] for g in range(heads)]
        return _online_softmax_heads(carry, fns, [vt_ref[g, c] for g in range(heads)])

    carry = lax.fori_loop(0, qi * per_q, body, tuple(carry))
    for g in range(heads):
        m, l, acc = carry[g]
        o_ref[:, g * dv:(g + 1) * dv] = (acc / l).T.astype(BF16)


def _mla_attn(q, k, v, B, S):
    H = MLA_HEADS
    G = MLA_HEADS_PER_STEP
    T = B * S
    tq = min(MLA_TQ, S)
    tk = min(MLA_TK, tq)
    nq = S // tq
    scale = (MLA_NOPE + MLA_ROPE) ** -0.5
    return pl.pallas_call(
        functools.partial(_mla_attn_kernel, tq=tq, tk=tk, scale=scale, heads=G),
        grid=(B, H // G, nq),
        in_specs=[pl.BlockSpec((tq, G * MLA_SLOT), lambda b, h, i: (b * nq + i, h)),
                  pl.BlockSpec((S, G * MLA_SLOT), lambda b, h, i: (b, h)),
                  pl.BlockSpec((S, G * MLA_V), lambda b, h, i: (b, h))],
        out_specs=pl.BlockSpec((tq, G * MLA_V), lambda b, h, i: (b * nq + i, h)),
        out_shape=SDS((T, H * MLA_V), BF16),
        scratch_shapes=[pltpu.VMEM((G, S // tk, MLA_V, tk), BF16)],
        compiler_params=_params(3),
        name="mla_attn",
    )(q, k, v)


def _t5_bucket(dist):
    n = jnp.maximum(dist, 0)
    max_exact = REL_BUCKETS // 2
    large = max_exact + (jnp.log(jnp.maximum(n, 1).astype(F32) / max_exact)
                         / math.log(REL_MAX_DIST / max_exact)
                         * (REL_BUCKETS - max_exact)).astype(I32)
    large = jnp.minimum(large, REL_BUCKETS - 1)
    return jnp.where(n < max_exact, n, large)


def _moba_bias_kernel(rb_ref, bidx_ref, o_ref):
    h = pl.program_id(0)
    for t in range(2):
        bi = bidx_ref[t]
        val = jnp.zeros(bi.shape, F32)
        for b in range(REL_BUCKETS):
            val = jnp.where(bi == b, rb_ref[b, h], val)
        o_ref[t] = val * LOG2E


def _moba_bias(rel_bias, bidx):
    H = rel_bias.shape[1]
    blk = MOBA_BLOCK
    return pl.pallas_call(
        _moba_bias_kernel,
        grid=(H,),
        in_specs=[pl.BlockSpec(memory_space=pltpu.SMEM),
                  pl.BlockSpec((2, blk, blk), lambda h: (0, 0, 0))],
        out_specs=pl.BlockSpec((None, 2, blk, blk), lambda h: (h, 0, 0, 0)),
        out_shape=SDS((H, 2, blk, blk), F32),
        compiler_params=_params(1),
        name="moba_bias",
    )(rel_bias, bidx)


def _moba_attn_kernel(rb_ref, q_ref, k_ref, v_ref, et_ref, bias_ref, o_ref, ka_ref, km_ref, qa_ref, vt_ref,
                      *, nb, n_sel, scale, heads):
    hg = pl.program_id(1)
    i = pl.program_id(2)
    blk = MOBA_BLOCK
    d = MOBA_HEAD_DIM
    nbp = -(-nb // 8) * 8

    @pl.when(i == 0)
    def _():
        for g in range(heads):
            cols = slice(g * d, (g + 1) * d)
            ka_ref[g, :, :d] = k_ref[:, cols]
            ka_ref[g, :, d:] = et_ref[...]
            _transpose_chunks(v_ref.at[:, cols], vt_ref.at[g])
            km_ref[g] = jnp.zeros(km_ref.shape[1:], F32)
            for n in range(nb):
                kb = k_ref[n * blk:(n + 1) * blk, cols].astype(F32)
                km_ref[g, n:n + 1, :] = jnp.sum(kb, axis=0, keepdims=True) * (1.0 / blk)

    qas = []
    for g in range(heads):
        q = q_ref[:, g * d:(g + 1) * d]
        gate = lax.dot_general(km_ref[g, :nbp, :].astype(BF16), q, _NT, preferred_element_type=F32)
        sub = lax.broadcasted_iota(I32, gate.shape, 0)
        gt = jnp.where(sub < i, gate, NEG_INF)
        keep = jnp.full(gate.shape, NEG_INF, F32)
        for _ in range(n_sel):
            mx = jnp.max(gt, axis=0, keepdims=True)
            first = jnp.min(jnp.where(gt == mx, sub, LANES), axis=0, keepdims=True)
            pick = sub == first
            keep = jnp.where(pick & (sub < i), 0.0, keep)
            gt = jnp.where(pick, -3.0e38, gt)
        keep = jnp.where(sub == i, 0.0, keep)
        keep = jnp.concatenate([keep, jnp.full((LANES - nbp, blk), NEG_INF, F32)], axis=0)
        qa_ref[g, :, :d] = q
        qa_ref[g, :, d:] = keep.T.astype(BF16)
        qas.append(qa_ref[g])

    def block(g, n):
        kk = ka_ref[g, pl.ds(pl.multiple_of(n * blk, blk), blk), :]
        return lax.dot_general(kk, qas[g], _NT, preferred_element_type=F32) * (scale * LOG2E), vt_ref[g, n]

    carry = []
    for g in range(heads):
        s, vt = block(g, i)
        s = s + bias_ref[g, 0]
        key = lax.broadcasted_iota(I32, s.shape, 0)
        qry = lax.broadcasted_iota(I32, s.shape, 1)
        s = jnp.where(key <= qry, s, NEG_INF)
        m = jnp.max(s, axis=0, keepdims=True)
        p = jnp.exp2(s - m)
        carry.append((m, jnp.sum(p, axis=0, keepdims=True), _dot(vt, p.astype(BF16))))

    def adjacent(n, carry):
        fns = [lambda g=g: block(g, n)[0] + bias_ref[g, 1] for g in range(heads)]
        return _online_softmax_heads(carry, fns, [vt_ref[g, n] for g in range(heads)])

    far_bias = [rb_ref[REL_BUCKETS - 1, hg * heads + g] * LOG2E for g in range(heads)]

    def far(n, carry):
        fns = [lambda g=g: block(g, n)[0] + far_bias[g] for g in range(heads)]
        return _online_softmax_heads(carry, fns, [vt_ref[g, n] for g in range(heads)])

    def far_pair(j, carry):
        def scores(g):
            kk = ka_ref[g, pl.ds(pl.multiple_of(j * (2 * blk), 2 * blk), 2 * blk), :]
            s = lax.dot_general(kk, qas[g], _NT, preferred_element_type=F32)
            return s * (scale * LOG2E) + far_bias[g]
        fns = [lambda g=g: scores(g) for g in range(heads)]
        return _online_softmax_heads(carry, fns, [(vt_ref[g, 2 * j], vt_ref[g, 2 * j + 1]) for g in range(heads)])

    n_far = jnp.maximum(i - 1, 0)
    n_pair = lax.shift_right_logical(n_far, 1)
    carry = lax.fori_loop(n_far, i, adjacent, tuple(carry))
    carry = lax.fori_loop(0, n_pair, far_pair, carry)
    carry = lax.fori_loop(2 * n_pair, n_far, far, carry)
    for g in range(heads):
        m, l, acc = carry[g]
        o_ref[:, g * d:(g + 1) * d] = (acc / l).T.astype(BF16)


def _moba_attn(qkv, et, bias, rel_bias, B, S):
    H, d, blk = MOBA_HEADS, MOBA_HEAD_DIM, MOBA_BLOCK
    T = B * S
    nb = S // blk
    n_sel = min(MOBA_TOPK, nb)
    G = MOBA_HEADS_PER_STEP
    ng = H // G
    return pl.pallas_call(
        functools.partial(_moba_attn_kernel, nb=nb, n_sel=n_sel, scale=d ** -0.5, heads=G),
        grid=(B, ng, nb),
        in_specs=[pl.BlockSpec(memory_space=pltpu.SMEM),
                  pl.BlockSpec((blk, G * d), lambda b, h, i: (b * nb + i, h)),
                  pl.BlockSpec((S, G * d), lambda b, h, i: (b, ng + h)),
                  pl.BlockSpec((S, G * d), lambda b, h, i: (b, 2 * ng + h)),
                  pl.BlockSpec((S, LANES), lambda b, h, i: (0, 0)),
                  pl.BlockSpec((G, 2, blk, blk), lambda b, h, i: (h, 0, 0, 0))],
        out_specs=pl.BlockSpec((blk, G * d), lambda b, h, i: (b * nb + i, h)),
        out_shape=SDS((T, H * d), BF16),
        scratch_shapes=[pltpu.VMEM((G, S, d + LANES), BF16),
                        pltpu.VMEM((G, LANES, d), F32),
                        pltpu.VMEM((G, blk, d + LANES), BF16),
                        pltpu.VMEM((G, nb, d, blk), BF16)],
        compiler_params=_params(3),
        name="moba_attn",
    )(rel_bias, qkv, qkv, qkv, et, bias)


def _out_proj_kernel(ya_ref, yb_ref, g_ref, x_ref, woa_ref, wob_ref, wout_ref, mg_ref, wr_ref, br_ref,
                     h1_ref, xn_ref, lg_ref, *, D):
    a = _dot(ya_ref[...], woa_ref[...])
    b = _dot(yb_ref[...], wob_ref[...])
    g = g_ref[...]
    merged = jax.nn.sigmoid(g[:, :D]) * a + jax.nn.sigmoid(g[:, D:]) * b
    h1 = x_ref[...] + _dot(merged.astype(BF16), wout_ref[...])
    h1_ref[...] = h1
    xn = _rms(h1, mg_ref[...]).astype(BF16)
    lg_ref[...] = _dot(xn, wr_ref[...]) + br_ref[...]
    xn_ref[...] = _pack_bf16_pairs(xn)


def _pack_bf16_pairs(x):
    bits = lax.bitcast_convert_type(x.astype(F32), U32)
    half = x.shape[1] // 2
    return (bits[:, :half] >> 16) | (bits[:, half:] & U32(HIGH_HALF))


def _unpack_bf16_pairs(w):
    lo = lax.bitcast_convert_type(w << 16, F32).astype(BF16)
    hi = lax.bitcast_convert_type(w & U32(HIGH_HALF), F32).astype(BF16)
    return lo, hi


def _resident(a):
    return pl.BlockSpec(a.shape, lambda i: (0,) * a.ndim, pipeline_mode=pl.Buffered(1))


def _out_proj(ya, yb, gates, x2, woa, wob, wout, mgain, wr, br):
    T, D = x2.shape
    tm = min(OUT_TM, T)
    row = lambda a: pl.BlockSpec((tm, a.shape[1]), lambda i: (i, 0))
    return pl.pallas_call(
        functools.partial(_out_proj_kernel, D=D),
        grid=(T // tm,),
        in_specs=[row(ya), row(yb), row(gates), row(x2),
                  _resident(woa), _resident(wob), _resident(wout), _resident(mgain),
                  _resident(wr), _resident(br)],
        out_specs=[pl.BlockSpec((tm, D), lambda i: (i, 0)),
                   pl.BlockSpec((tm, D // 2), lambda i: (i, 0)),
                   pl.BlockSpec((tm, LANES), lambda i: (i, 0))],
        out_shape=[SDS((T, D), F32), SDS((T, D // 2), U32), SDS((T, LANES), F32)],
        compiler_params=_params(1),
        name="out_proj",
    )(ya, yb, gates, x2, woa, wob, wout, mgain, wr, br)


def _route_kernel(lg_ref, e_ref, w_ref, r_ref, cnt_ref, carry_ref, *, n_exp):
    i = pl.program_id(0)

    @pl.when(i == 0)
    def _():
        carry_ref[...] = jnp.zeros(carry_ref.shape, F32)

    lt = lg_ref[...].T
    tm = lt.shape[1]
    sub = lax.broadcasted_iota(I32, lt.shape, 0)
    neg = -jnp.inf
    cur = jnp.where(sub < n_exp, lt, neg)
    vals, idxs = [], []
    for _ in range(TOP_K):
        mx = jnp.max(cur, axis=0, keepdims=True)
        ix = jnp.min(jnp.where(cur == mx, sub, LANES), axis=0, keepdims=True)
        vals.append(mx)
        idxs.append(ix)
        cur = jnp.where(sub == ix, neg, cur)
    ex = [jnp.exp(v - vals[0]) for v in vals]
    den = ex[0]
    for e in ex[1:]:
        den = den + e
    onehot = jnp.zeros(lt.shape, F32)
    for ix in idxs:
        onehot = onehot + (sub == ix).astype(F32)
    r_i = lax.broadcasted_iota(I32, (tm, tm), 0)
    c_i = lax.broadcasted_iota(I32, (tm, tm), 1)
    tri = (r_i <= c_i).astype(BF16)
    incl = _dot(onehot.astype(BF16), tri)
    base = carry_ref[...]
    excl = incl - onehot + base[:, :1]
    e_ref[...] = jnp.zeros(e_ref.shape, I32)
    w_ref[...] = jnp.zeros(w_ref.shape, F32)
    r_ref[...] = jnp.zeros(r_ref.shape, I32)
    for k in range(TOP_K):
        e_ref[k:k + 1, :] = idxs[k]
        w_ref[k:k + 1, :] = ex[k] / den
        rk = jnp.sum(jnp.where(sub == idxs[k], excl, 0.0), axis=0, keepdims=True)
        r_ref[k:k + 1, :] = rk.astype(I32)
    total = base + jnp.sum(onehot, axis=1, keepdims=True)
    carry_ref[...] = total
    cnt_ref[...] = total


def _route(logits, n_exp):
    T = logits.shape[0]
    tm = min(ROUTE_TM, T)
    return pl.pallas_call(
        functools.partial(_route_kernel, n_exp=n_exp),
        grid=(T // tm,),
        in_specs=[pl.BlockSpec((tm, LANES), lambda i: (i, 0))],
        out_specs=[pl.BlockSpec((8, tm), lambda i: (0, i)),
                   pl.BlockSpec((8, tm), lambda i: (0, i)),
                   pl.BlockSpec((8, tm), lambda i: (0, i)),
                   pl.BlockSpec((LANES, LANES), lambda i: (0, 0))],
        out_shape=[SDS((8, T), I32), SDS((8, T), F32), SDS((8, T), I32), SDS((LANES, LANES), F32)],
        scratch_shapes=[pltpu.VMEM((LANES, LANES), F32)],
        compiler_params=_params(1),
        name="route",
    )(logits)


def _dest_kernel(e_ref, r_ref, base_ref, d_ref):
    base = base_ref[...][:, :1]
    sub = lax.broadcasted_iota(I32, (LANES, e_ref.shape[1]), 0)
    d_ref[...] = jnp.zeros(d_ref.shape, I32)
    for k in range(TOP_K):
        off = jnp.sum(jnp.where(sub == e_ref[k:k + 1, :], base, 0), axis=0, keepdims=True)
        d_ref[k:k + 1, :] = r_ref[k:k + 1, :] + off


def _dest(e_k, r_k, base):
    T = e_k.shape[1]
    tb = min(DEST_TB, T)
    blk = pl.BlockSpec((8, tb), lambda i: (0, i))
    return pl.pallas_call(
        _dest_kernel,
        grid=(T // tb,),
        in_specs=[blk, blk, pl.BlockSpec((LANES, LANES), lambda i: (0, 0))],
        out_specs=blk,
        out_shape=SDS((8, T), I32),
        compiler_params=_params(1),
        name="dest",
    )(e_k, r_k, base)


def _dispatch_kernel(zflag_ref, dest_ref, x_ref, xg_hbm, zbuf_ref, sem, *, tm, rows, n_tiles):
    i = pl.program_id(0)

    @pl.when(i == 0)
    def _():
        zbuf_ref[...] = jnp.zeros(zbuf_ref.shape, zbuf_ref.dtype)

        def zcopy(j):
            start = pl.multiple_of(j * rows, rows)
            return pltpu.make_async_copy(zbuf_ref, xg_hbm.at[pl.ds(start, rows)], sem)

        def zstart(j, c):
            @pl.when(zflag_ref[j] != 0)
            def _():
                zcopy(j).start()
            return c

        def zwait(j, c):
            @pl.when(zflag_ref[j] != 0)
            def _():
                zcopy(j).wait()
            return c

        lax.fori_loop(0, n_tiles, zstart, 0)
        lax.fori_loop(0, n_tiles, zwait, 0)

    def copy(t, k):
        return pltpu.make_async_copy(x_ref.at[pl.ds(t, 1)], xg_hbm.at[pl.ds(dest_ref[k, t], 1)], sem)

    def start(t, c):
        for k in range(TOP_K):
            copy(t, k).start()
        return c

    def wait(t, c):
        for k in range(TOP_K):
            copy(t, k).wait()
        return c

    lax.fori_loop(0, tm, start, 0, unroll=4)
    lax.fori_loop(0, tm, wait, 0, unroll=4)


def _dispatch(zflag, dest, xn):
    T, D = xn.shape
    tm = min(DISPATCH_TM, T)
    n_chunks = zflag.shape[0]
    return pl.pallas_call(
        functools.partial(_dispatch_kernel, tm=tm, rows=ZERO_ROWS, n_tiles=n_chunks),
        grid_spec=pltpu.PrefetchScalarGridSpec(
            num_scalar_prefetch=1,
            grid=(T // tm,),
            in_specs=[pl.BlockSpec((8, tm), lambda i, zf: (0, i), memory_space=pltpu.SMEM),
                      pl.BlockSpec((tm, D), lambda i, zf: (i, 0))],
            out_specs=pl.BlockSpec(memory_space=pl.ANY),
            scratch_shapes=[pltpu.VMEM((ZERO_ROWS, D), xn.dtype), pltpu.SemaphoreType.DMA]),
        out_shape=SDS((n_chunks * ZERO_ROWS, D), xn.dtype),
        compiler_params=_params(1),
        name="dispatch",
    )(zflag, dest, xn)


def _row_blocks(valid, n_rows, compute, out_ref):
    full = valid > n_rows - EXPERT_SUB

    @pl.when(full)
    def _():
        compute(slice(0, n_rows))

    for sb in range(n_rows // EXPERT_SUB):
        rows = slice(sb * EXPERT_SUB, (sb + 1) * EXPERT_SUB)

        @pl.when(jnp.logical_not(full) & (sb * EXPERT_SUB < valid))
        def _():
            compute(rows)

        @pl.when(jnp.logical_not(full) & (sb * EXPERT_SUB >= valid))
        def _():
            out_ref[rows, :] = jnp.zeros((EXPERT_SUB, out_ref.shape[1]), out_ref.dtype)


def _ffn_up_kernel(te_ref, tv_ref, nu_ref, x_ref, wg_ref, wu_ref, bg_ref, bu_ref, h_ref, wgb_ref, wub_ref):
    i = pl.program_id(1)
    valid = tv_ref[i]
    new_expert = (i == 0) | (te_ref[i] != te_ref[jnp.maximum(i - 1, 0)])

    @pl.when((valid > 0) & new_expert)
    def _():
        wgb_ref[...] = wg_ref[...].astype(BF16)
        wub_ref[...] = wu_ref[...].astype(BF16)

    half = x_ref.shape[1]

    def compute(rows):
        lo, hi = _unpack_bf16_pairs(x_ref[rows, :])
        g = _dot(lo, wgb_ref[:half, :]) + _dot(hi, wgb_ref[half:, :]) + bg_ref[...]
        u = _dot(lo, wub_ref[:half, :]) + _dot(hi, wub_ref[half:, :]) + bu_ref[...]
        g = jnp.minimum(g, SWIGLU_LIMIT)
        u = jnp.clip(u, -SWIGLU_LIMIT, SWIGLU_LIMIT)
        h_ref[rows, :] = (g * jax.nn.sigmoid(SWIGLU_ALPHA * g) * (u + 1.0)).astype(BF16)

    _row_blocks(valid, x_ref.shape[0], compute, h_ref)


def _ffn_up(tile_expert, tile_valid, n_used, xg, wg, wu, bg, bu):
    P = xg.shape[0]
    D, Dx = wg.shape[1], wg.shape[2]
    tr = EXPERT_ROWS
    tn = min(EXPERT_TN_UP, Dx)
    n_tiles = P // tr
    xmap = lambda j, i, te, tv, nu: (jnp.minimum(i, nu[0] - 1), 0)
    wmap = lambda j, i, te, tv, nu: (te[i], 0, j)
    return pl.pallas_call(
        _ffn_up_kernel,
        grid_spec=pltpu.PrefetchScalarGridSpec(
            num_scalar_prefetch=3,
            grid=(Dx // tn, n_tiles),
            in_specs=[pl.BlockSpec((tr, D // 2), xmap),
                      pl.BlockSpec((None, D, tn), wmap),
                      pl.BlockSpec((None, D, tn), wmap),
                      pl.BlockSpec((None, 1, tn), wmap),
                      pl.BlockSpec((None, 1, tn), wmap)],
            out_specs=pl.BlockSpec((tr, tn), lambda j, i, te, tv, nu: (i, j)),
            scratch_shapes=[pltpu.VMEM((D, tn), BF16), pltpu.VMEM((D, tn), BF16)]),
        out_shape=SDS((P, Dx), BF16),
        compiler_params=_params(2),
        name="ffn_up",
    )(tile_expert, tile_valid, n_used, xg, wg, wu, bg, bu)


def _ffn_down_kernel(te_ref, tv_ref, nu_ref, h_ref, wd_ref, bd_ref, y_ref, wdb_ref):
    i = pl.program_id(1)
    valid = tv_ref[i]
    new_expert = (i == 0) | (te_ref[i] != te_ref[jnp.maximum(i - 1, 0)])

    @pl.when((valid > 0) & new_expert)
    def _():
        wdb_ref[...] = wd_ref[...].astype(BF16)

    def compute(rows):
        y = _dot(h_ref[rows, :], wdb_ref[...]) + bd_ref[...]
        y_ref[rows, :] = _pack_bf16_pairs(y.astype(BF16))

    _row_blocks(valid, h_ref.shape[0], compute, y_ref)


def _ffn_down(tile_expert, tile_valid, n_used, hid, wd, bd):
    P, Dx = hid.shape
    D = wd.shape[2]
    tr = EXPERT_ROWS
    tn = min(EXPERT_TN_DOWN, D)
    n_tiles = P // tr
    hmap = lambda j, i, te, tv, nu: (jnp.minimum(i, nu[0] - 1), 0)
    wmap = lambda j, i, te, tv, nu: (te[i], 0, j)
    return pl.pallas_call(
        _ffn_down_kernel,
        grid_spec=pltpu.PrefetchScalarGridSpec(
            num_scalar_prefetch=3,
            grid=(D // tn, n_tiles),
            in_specs=[pl.BlockSpec((tr, Dx), hmap),
                      pl.BlockSpec((None, Dx, tn), wmap),
                      pl.BlockSpec((None, 1, tn), wmap)],
            out_specs=pl.BlockSpec((tr, tn // 2), lambda j, i, te, tv, nu: (i, j)),
            scratch_shapes=[pltpu.VMEM((Dx, tn), BF16)]),
        out_shape=SDS((P, D // 2), U32),
        compiler_params=_params(2),
        name="ffn_down",
    )(tile_expert, tile_valid, n_used, hid, wd, bd)


def _combine_kernel(dest_ref, w_ref, h1_ref, p_ref, wple_ref, wpg_ref, pg_ref, fg_ref, y_hbm,
                    o_ref, ybuf_ref, sem, *, tm, pack, final):
    def copy(t, k):
        return pltpu.make_async_copy(y_hbm.at[pl.ds(dest_ref[k, t], 1)],
                                     ybuf_ref.at[k, pl.ds(t, 1)], sem)

    def start(t, c):
        for k in range(TOP_K):
            copy(t, k).start()
        return c

    def wait(t, c):
        for k in range(TOP_K):
            copy(t, k).wait()
        return c

    lax.fori_loop(0, tm, start, 0, unroll=4)
    pw = _dot(p_ref[...].astype(BF16), wple_ref[...])
    lax.fori_loop(0, tm, wait, 0, unroll=4)
    w = w_ref[...]
    parts = []
    for c in range(ybuf_ref.shape[2] // pack):
        cols = slice(c * pack, (c + 1) * pack)
        lo = hi = None
        for k in range(TOP_K):
            words = ybuf_ref[k, :, cols]
            lo_k = w[:, k:k + 1] * lax.bitcast_convert_type(words << 16, F32)
            hi_k = w[:, k:k + 1] * lax.bitcast_convert_type(words & U32(HIGH_HALF), F32)
            lo = lo_k if lo is None else lo + lo_k
            hi = hi_k if hi is None else hi + hi_k
        parts += [lo, hi]
    h2 = h1_ref[...] + jnp.concatenate(parts, axis=1)
    xn = _rms(h2, pg_ref[...]).astype(BF16)
    gate = jax.nn.sigmoid(_dot(xn, wpg_ref[...]))
    h3 = h2 + pw * gate
    o_ref[...] = _rms(h3, fg_ref[...]) if final else h3


def _combine(dest, wts, h1, p2, wple, wpg, pgain, fgain, yg, final):
    T, D = h1.shape
    tm = min(COMBINE_TM, T)
    row = lambda a: pl.BlockSpec((tm, a.shape[1]), lambda i: (i, 0))
    return pl.pallas_call(
        functools.partial(_combine_kernel, tm=tm, pack=min(EXPERT_TN_DOWN, D) // 2, final=final),
        grid=(T // tm,),
        in_specs=[pl.BlockSpec((8, tm), lambda i: (0, i), memory_space=pltpu.SMEM),
                  row(wts), row(h1), row(p2),
                  _resident(wple), _resident(wpg), _resident(pgain), _resident(fgain),
                  pl.BlockSpec(memory_space=pl.ANY)],
        out_specs=pl.BlockSpec((tm, D), lambda i: (i, 0)),
        out_shape=SDS((T, D), F32),
        scratch_shapes=[pltpu.VMEM((TOP_K, tm, D // 2), U32), pltpu.SemaphoreType.DMA],
        compiler_params=_params(1),
        name="combine",
    )(dest, wts, h1, p2, wple, wpg, pgain, fgain, yg)


def kernel(x, p, attn_norm, w_in, q_lat_norm, kv_lat_norm, w_uq, w_ukv, w_o_mla, w_o_moba, w_out,
           rel_bias, moe_norm, w_router, b_router, w_gate, b_gate, w_up, b_up, w_down, b_down,
           ple_norm, w_ple_gate, w_ple, final_norm):
    B, S, D = x.shape
    T = B * S
    n_layers = w_in.shape[0]
    E = w_router.shape[-1]
    H = MLA_HEADS
    mw = MOBA_HEADS * MOBA_HEAD_DIM
    assert S % MOBA_BLOCK == 0 and E <= LANES
    o_kr = MLA_Q_LORA + MLA_KV_LORA
    o_q = o_kr + MLA_ROPE
    o_g = o_q + 3 * mw

    inv = 1.0 / (ROPE_THETA ** (jnp.arange(0, MLA_ROPE, 2, dtype=F32) / MLA_ROPE))
    ang = jnp.arange(S, dtype=F32)[:, None] * inv[None, :]
    cos, sin = jnp.cos(ang), jnp.sin(ang)
    zpad = jnp.zeros((S, LANES - MLA_ROPE), F32)
    cosw = jnp.concatenate([cos, cos, zpad], axis=1)
    sinw = jnp.concatenate([-sin, sin, zpad], axis=1)
    r = jnp.arange(MOBA_BLOCK)
    d0 = r[None, :] - r[:, None]
    bidx = jnp.stack([_t5_bucket(d0), _t5_bucket(d0 + MOBA_BLOCK)]).astype(I32)
    et = (jnp.arange(S)[:, None] // MOBA_BLOCK == jnp.arange(LANES)[None, :]).astype(BF16)
    bias = _moba_bias(rel_bias, bidx)

    h = x.reshape(T, D)
    for li in range(n_layers):
        w = w_in[li]
        w_main = jnp.concatenate([w[:, :o_kr], w[:, o_q:]], axis=1).astype(BF16)
        w_kr = jnp.pad(w[:, o_kr:o_q], ((0, 0), (0, LANES - MLA_ROPE))).astype(BF16)
        wq = w_uq[li].reshape(MLA_Q_LORA, H, MLA_NOPE + MLA_ROPE)
        wq = jnp.pad(wq, ((0, 0), (0, 0), (0, MLA_SLOT - MLA_NOPE - MLA_ROPE)))
        wq = wq.reshape(MLA_Q_LORA, H * MLA_SLOT).astype(BF16)
        wkv = w_ukv[li].reshape(MLA_KV_LORA, H, MLA_NOPE + MLA_V)
        wk = wkv[:, :, :MLA_NOPE].reshape(MLA_KV_LORA, H * MLA_NOPE).astype(BF16)
        wv = wkv[:, :, MLA_NOPE:].reshape(MLA_KV_LORA, H * MLA_V).astype(BF16)
        wr = jnp.pad(w_router[li], ((0, 0), (0, LANES - E))).astype(BF16)
        br = jnp.pad(b_router[li], (0, LANES - E), constant_values=NEG_INF)[None, :]

        lat, qkv, gates, kr = _in_proj(h, attn_norm[li][None, :], w_main, w_kr, o_kr, 3 * mw)
        q_a, k_a, v_a = _mla_proj(lat, kr, cosw, sinw, q_lat_norm[li][None, :], kv_lat_norm[li][None, :],
                                  wq, wk, wv, S)
        y_a = _mla_attn(q_a, k_a, v_a, B, S)
        y_b = _moba_attn(qkv, et, bias, rel_bias, B, S)
        h1, xn, logits = _out_proj(y_a, y_b, gates, h, w_o_mla[li].astype(BF16), w_o_moba[li].astype(BF16),
                                   w_out[li].astype(BF16), moe_norm[li][None, :], wr, br)

        e_k, w_k, r_k, cnt = _route(logits, E)
        counts = cnt[:E, 0].astype(I32)
        tiles = (counts + EXPERT_ROWS - 1) // EXPERT_ROWS
        tile_end = jnp.cumsum(tiles)
        tile_start = tile_end - tiles
        n_tiles = (T * TOP_K) // EXPERT_ROWS + E
        n_used = tile_end[-1]
        tile_ids = jnp.arange(n_tiles)
        capped = jnp.minimum(tile_ids, n_used - 1)
        tile_expert = jnp.minimum(jnp.sum(tile_end[None, :] <= capped[:, None], axis=1), E - 1).astype(I32)
        base = jnp.pad(tile_start * EXPERT_ROWS, (0, LANES - E)).astype(I32)
        dest = _dest(e_k, r_k, jnp.broadcast_to(base[:, None], (LANES, LANES)))
        mine = tile_expert[:, None] == jnp.arange(E)[None, :]
        in_tile = jnp.sum(jnp.where(mine, counts[None, :] - (tile_ids[:, None] - tile_start[None, :]) * EXPERT_ROWS, 0),
                          axis=1)
        tile_valid = jnp.where(tile_ids < n_used, jnp.clip(in_tile, 0, EXPERT_ROWS), 0).astype(I32)
        chunk_lo = jnp.arange(n_tiles * EXPERT_ROWS // ZERO_ROWS) * ZERO_ROWS
        pad_lo = tile_start * EXPERT_ROWS + counts
        pad_hi = tile_end * EXPERT_ROWS
        in_pad = (chunk_lo[:, None] < pad_hi[None, :]) & (chunk_lo[:, None] + ZERO_ROWS > pad_lo[None, :])
        zflag = (jnp.any(in_pad, axis=1) | (chunk_lo >= n_used * EXPERT_ROWS)).astype(I32)
        nu = n_used.astype(I32)[None]

        xg = _dispatch(zflag, dest, xn)
        hid = _ffn_up(tile_expert, tile_valid, nu, xg, w_gate[li], w_up[li],
                      b_gate[li][:, None, :], b_up[li][:, None, :])
        yg = _ffn_down(tile_expert, tile_valid, nu, hid, w_down[li], b_down[li][:, None, :])
        h = _combine(dest, w_k[:TOP_K].T, h1, p[li].reshape(T, -1), w_ple[li].astype(BF16),
                     w_ple_gate[li].astype(BF16), ple_norm[li][None, :], final_norm[None, :], yg,
                     final=li == n_layers - 1)
    return h.reshape(B, S, D)
```

```python
import functools
import math

import jax
import jax.numpy as jnp
from jax import lax
from jax.experimental import pallas as pl
from jax.experimental.pallas import tpu as pltpu

F32 = jnp.float32
BF16 = jnp.bfloat16
I32 = jnp.int32
U32 = jnp.uint32
HIGH_HALF = 0xFFFF0000
SDS = jax.ShapeDtypeStruct

EPS = 1e-6
NEG_INF = -1e30
LOG2E = math.log2(math.e)
MLA_HEADS = 8
MLA_NOPE = 128
MLA_ROPE = 64
MLA_V = 128
MLA_Q_LORA = 512
MLA_KV_LORA = 512
ROPE_THETA = 10000.0
MOBA_HEADS = 8
MOBA_HEAD_DIM = 128
MOBA_BLOCK = 256
MOBA_TOPK = 3
REL_BUCKETS = 32
REL_MAX_DIST = 128
TOP_K = 4
SWIGLU_LIMIT = 7.0
SWIGLU_ALPHA = 1.702

LANES = 128
MLA_SLOT = 2 * LANES
VMEM_LIMIT = 56 * 2**20

IN_TM, IN_TN = 1024, 512
MLAP_TM = 512
MLA_TQ = 512
MLA_TK = 512
MLA_HEADS_PER_STEP = 4
MOBA_HEADS_PER_STEP = 4
OUT_TM = 256
ROUTE_TM = 512
EXPERT_ROWS = 1024
EXPERT_SUB = 256
EXPERT_TN_UP = 512
EXPERT_TN_DOWN = 1024
ZERO_ROWS = 512
DISPATCH_TM = 512
DEST_TB = 2048
COMBINE_TM = 512

_NT = (((1,), (1,)), ((), ()))


def _params(n_axes):
    return pltpu.CompilerParams(dimension_semantics=("arbitrary",) * n_axes,
                                vmem_limit_bytes=VMEM_LIMIT)


def _rms(x, g):
    return x * lax.rsqrt(jnp.mean(x * x, axis=-1, keepdims=True) + EPS) * g


def _dot(a, b):
    return jnp.dot(a, b, preferred_element_type=F32)


def _in_proj_kernel(x_ref, g_ref, w_ref, wkr_ref, lat_ref, qkv_ref, gates_ref, kr_ref, xn_ref,
                    *, n_lat, n_qkv):
    j = pl.program_id(1)

    @pl.when(j == 0)
    def _():
        xn = _rms(x_ref[...], g_ref[...]).astype(BF16)
        xn_ref[...] = xn
        kr_ref[...] = _dot(xn, wkr_ref[...])

    @pl.when(j < n_lat)
    def _():
        lat_ref[...] = _dot(xn_ref[...], w_ref[...])

    @pl.when((j >= n_lat) & (j < n_lat + n_qkv))
    def _():
        qkv_ref[...] = _dot(xn_ref[...], w_ref[...]).astype(BF16)

    @pl.when(j >= n_lat + n_qkv)
    def _():
        gates_ref[...] = _dot(xn_ref[...], w_ref[...])


def _in_proj(x2, gain, w_main, w_kr, n_lat_cols, n_qkv_cols):
    T, D = x2.shape
    n_g_cols = w_main.shape[1] - n_lat_cols - n_qkv_cols
    tm = min(IN_TM, T)
    tn = math.gcd(math.gcd(IN_TN, n_lat_cols), math.gcd(n_qkv_cols, n_g_cols))
    n_lat, n_qkv, n_g = n_lat_cols // tn, n_qkv_cols // tn, n_g_cols // tn
    return pl.pallas_call(
        functools.partial(_in_proj_kernel, n_lat=n_lat, n_qkv=n_qkv),
        grid=(T // tm, n_lat + n_qkv + n_g),
        in_specs=[pl.BlockSpec((tm, D), lambda i, j: (i, 0)),
                  pl.BlockSpec((1, D), lambda i, j: (0, 0)),
                  pl.BlockSpec((D, tn), lambda i, j: (0, j)),
                  pl.BlockSpec((D, LANES), lambda i, j: (0, 0))],
        out_specs=[pl.BlockSpec((tm, tn), lambda i, j: (i, jnp.minimum(j, n_lat - 1))),
                   pl.BlockSpec((tm, tn), lambda i, j: (i, jnp.clip(j - n_lat, 0, n_qkv - 1))),
                   pl.BlockSpec((tm, tn), lambda i, j: (i, jnp.maximum(j - n_lat - n_qkv, 0))),
                   pl.BlockSpec((tm, LANES), lambda i, j: (i, 0))],
        out_shape=[SDS((T, n_lat_cols), F32), SDS((T, n_qkv_cols), BF16),
                   SDS((T, n_g_cols), F32), SDS((T, LANES), F32)],
        scratch_shapes=[pltpu.VMEM((tm, D), BF16)],
        compiler_params=_params(2),
        name="in_proj",
    )(x2, gain, w_main, w_kr)


def _mla_proj_kernel(lat_ref, kr_ref, cos_ref, sin_ref, qn_ref, kvn_ref, wq_ref, wk_ref, wv_ref,
                     q_ref, k_ref, v_ref):
    lat = lat_ref[...]
    qn = _rms(lat[:, :MLA_Q_LORA], qn_ref[...]).astype(BF16)
    kvn = _rms(lat[:, MLA_Q_LORA:], kvn_ref[...]).astype(BF16)
    q = _dot(qn, wq_ref[...])
    kn = _dot(kvn, wk_ref[...])
    v_ref[...] = _dot(kvn, wv_ref[...]).astype(BF16)
    c = cos_ref[...]
    s = sin_ref[...]
    half = MLA_ROPE // 2
    lane = lax.broadcasted_iota(I32, c.shape, 1)

    def rope(xr):
        swapped = jnp.where(lane < half, pltpu.roll(xr, LANES - half, 1), pltpu.roll(xr, half, 1))
        return xr * c + swapped * s

    kr = rope(kr_ref[...]).astype(BF16)
    for h in range(MLA_HEADS):
        lo = h * MLA_SLOT
        q_ref[:, lo:lo + LANES] = q[:, lo:lo + LANES].astype(BF16)
        q_ref[:, lo + LANES:lo + MLA_SLOT] = rope(q[:, lo + LANES:lo + MLA_SLOT]).astype(BF16)
        k_ref[:, lo:lo + LANES] = kn[:, h * MLA_NOPE:(h + 1) * MLA_NOPE].astype(BF16)
        k_ref[:, lo + LANES:lo + MLA_SLOT] = kr


def _mla_proj(lat, kr, cosw, sinw, qnorm, kvnorm, wq, wk, wv, S):
    T = lat.shape[0]
    tm = min(MLAP_TM, S)
    ns = S // tm
    H = MLA_HEADS
    full = lambda a: pl.BlockSpec(a.shape, lambda i: (0,) * a.ndim)
    return pl.pallas_call(
        _mla_proj_kernel,
        grid=(T // tm,),
        in_specs=[pl.BlockSpec((tm, lat.shape[1]), lambda i: (i, 0)),
                  pl.BlockSpec((tm, LANES), lambda i: (i, 0)),
                  pl.BlockSpec((tm, LANES), lambda i: (i % ns, 0)),
                  pl.BlockSpec((tm, LANES), lambda i: (i % ns, 0)),
                  full(qnorm), full(kvnorm), full(wq), full(wk), full(wv)],
        out_specs=[pl.BlockSpec((tm, H * MLA_SLOT), lambda i: (i, 0)),
                   pl.BlockSpec((tm, H * MLA_SLOT), lambda i: (i, 0)),
                   pl.BlockSpec((tm, H * MLA_V), lambda i: (i, 0))],
        out_shape=[SDS((T, H * MLA_SLOT), BF16), SDS((T, H * MLA_SLOT), BF16),
                   SDS((T, H * MLA_V), BF16)],
        compiler_params=_params(1),
        name="mla_proj",
    )(lat, kr, cosw, sinw, qnorm, kvnorm, wq, wk, wv)


def _transpose_chunks(src_ref, dst_ref):
    n, _, tk = dst_ref.shape
    for c in range(n):
        dst_ref[c] = src_ref[c * tk:(c + 1) * tk, :].astype(F32).T.astype(dst_ref.dtype)


def _online_softmax_t(carry, s, vt):
    return _online_softmax_heads((carry,), (lambda: s,), (vt,))[0]


def _online_softmax_heads(carries, score_fns, vts):
    n = len(carries)
    scores, stats, out = {}, {}, [None] * n
    for step in range(n + 2):
        if step < n:
            scores[step] = score_fns[step]()
        g = step - 1
        if 0 <= g < n:
            m, l, _ = carries[g]
            s = scores.pop(g)
            m_new = jnp.maximum(m, jnp.max(s, axis=0, keepdims=True))
            alpha = jnp.exp2(m - m_new)
            p = jnp.exp2(s - m_new)
            stats[g] = (m_new, alpha * l + jnp.sum(p, axis=0, keepdims=True), alpha, p.astype(BF16))
        g = step - 2
        if 0 <= g < n:
            m_new, l, alpha, p = stats.pop(g)
            chunks = vts[g] if isinstance(vts[g], (tuple, list)) else (vts[g],)
            rows = p.shape[0] // len(chunks)
            pv = _dot(chunks[0], p[:rows])
            for c in range(1, len(chunks)):
                pv = pv + _dot(chunks[c], p[c * rows:(c + 1) * rows])
            out[g] = (m_new, l, alpha * carries[g][2] + pv)
    return tuple(out)


def _mla_attn_kernel(q_ref, k_ref, v_ref, o_ref, vt_ref, *, tq, tk, scale, heads):
    qi = pl.program_id(2)
    slot, dv = MLA_SLOT, MLA_V

    @pl.when(qi == 0)
    def _():
        for g in range(heads):
            _transpose_chunks(v_ref.at[:, g * dv:(g + 1) * dv], vt_ref.at[g])

    qs = [q_ref[:, g * slot:(g + 1) * slot] for g in range(heads)]
    per_q = tq // tk

    def chunk(g, c):
        k = k_ref[pl.ds(pl.multiple_of(c * tk, tk), tk), g * slot:(g + 1) * slot]
        return lax.dot_general(k, qs[g], _NT, preferred_element_type=F32) * (scale * LOG2E), vt_ref[g, c]

    carry = [None] * heads
    for c in range(per_q):
        for g in range(heads):
            s, vt = chunk(g, qi * per_q + c)
            key = lax.broadcasted_iota(I32, s.shape, 0) + c * tk
            qry = lax.broadcasted_iota(I32, s.shape, 1)
            s = jnp.where(key <= qry, s, NEG_INF)
            if carry[g] is None:
                m = jnp.max(s, axis=0, keepdims=True)
                p = jnp.exp2(s - m)
                carry[g] = (m, jnp.sum(p, axis=0, keepdims=True), _dot(vt, p.astype(BF16)))
            else:
                carry[g] = _online_softmax_t(carry[g], s, vt)

    def body(c, carry):
        fns = [lambda g=g: chunk(g, c)[0] for g in range(heads)]
        return _online_softmax_heads(carry, fns, [vt_ref[g, c] for g in range(heads)])

    carry = lax.fori_loop(0, qi * per_q, body, tuple(carry))
    for g in range(heads):
        m, l, acc = carry[g]
        o_ref[:, g * dv:(g + 1) * dv] = (acc / l).T.astype(BF16)


def _mla_attn(q, k, v, B, S):
    H = MLA_HEADS
    G = MLA_HEADS_PER_STEP
    T = B * S
    tq = min(MLA_TQ, S)
    tk = min(MLA_TK, tq)
    nq = S // tq
    scale = (MLA_NOPE + MLA_ROPE) ** -0.5
    return pl.pallas_call(
        functools.partial(_mla_attn_kernel, tq=tq, tk=tk, scale=scale, heads=G),
        grid=(B, H // G, nq),
        in_specs=[pl.BlockSpec((tq, G * MLA_SLOT), lambda b, h, i: (b * nq + i, h)),
                  pl.BlockSpec((S, G * MLA_SLOT), lambda b, h, i: (b, h)),
                  pl.BlockSpec((S, G * MLA_V), lambda b, h, i: (b, h))],
        out_specs=pl.BlockSpec((tq, G * MLA_V), lambda b, h, i: (b * nq + i, h)),
        out_shape=SDS((T, H * MLA_V), BF16),
        scratch_shapes=[pltpu.VMEM((G, S // tk, MLA_V, tk), BF16)],
        compiler_params=_params(3),
        name="mla_attn",
    )(q, k, v)


def _t5_bucket(dist):
    n = jnp.maximum(dist, 0)
    max_exact = REL_BUCKETS // 2
    large = max_exact + (jnp.log(jnp.maximum(n, 1).astype(F32) / max_exact)
                         / math.log(REL_MAX_DIST / max_exact)
                         * (REL_BUCKETS - max_exact)).astype(I32)
    large = jnp.minimum(large, REL_BUCKETS - 1)
    return jnp.where(n < max_exact, n, large)


def _moba_bias_kernel(rb_ref, bidx_ref, o_ref):
    h = pl.program_id(0)
    for t in range(2):
        bi = bidx_ref[t]
        val = jnp.zeros(bi.shape, F32)
        for b in range(REL_BUCKETS):
            val = jnp.where(bi == b, rb_ref[b, h], val)
        o_ref[t] = val * LOG2E


def _moba_bias(rel_bias, bidx):
    H = rel_bias.shape[1]
    blk = MOBA_BLOCK
    return pl.pallas_call(
        _moba_bias_kernel,
        grid=(H,),
        in_specs=[pl.BlockSpec(memory_space=pltpu.SMEM),
                  pl.BlockSpec((2, blk, blk), lambda h: (0, 0, 0))],
        out_specs=pl.BlockSpec((None, 2, blk, blk), lambda h: (h, 0, 0, 0)),
        out_shape=SDS((H, 2, blk, blk), F32),
        compiler_params=_params(1),
        name="moba_bias",
    )(rel_bias, bidx)


def _moba_attn_kernel(rb_ref, q_ref, k_ref, v_ref, et_ref, bias_ref, o_ref, ka_ref, km_ref, qa_ref, vt_ref,
                      *, nb, n_sel, scale, heads):
    hg = pl.program_id(1)
    i = pl.program_id(2)
    blk = MOBA_BLOCK
    d = MOBA_HEAD_DIM
    nbp = -(-nb // 8) * 8

    @pl.when(i == 0)
    def _():
        for g in range(heads):
            cols = slice(g * d, (g + 1) * d)
            ka_ref[g, :, :d] = k_ref[:, cols]
            ka_ref[g, :, d:] = et_ref[...]
            _transpose_chunks(v_ref.at[:, cols], vt_ref.at[g])
            km_ref[g] = jnp.zeros(km_ref.shape[1:], F32)
            for n in range(nb):
                kb = k_ref[n * blk:(n + 1) * blk, cols].astype(F32)
                km_ref[g, n:n + 1, :] = jnp.sum(kb, axis=0, keepdims=True) * (1.0 / blk)

    qas = []
    for g in range(heads):
        q = q_ref[:, g * d:(g + 1) * d]
        gate = lax.dot_general(km_ref[g, :nbp, :].astype(BF16), q, _NT, preferred_element_type=F32)
        sub = lax.broadcasted_iota(I32, gate.shape, 0)
        gt = jnp.where(sub < i, gate, NEG_INF)
        keep = jnp.full(gate.shape, NEG_INF, F32)
        for _ in range(n_sel):
            mx = jnp.max(gt, axis=0, keepdims=True)
            first = jnp.min(jnp.where(gt == mx, sub, LANES), axis=0, keepdims=True)
            pick = sub == first
            keep = jnp.where(pick & (sub < i), 0.0, keep)
            gt = jnp.where(pick, -3.0e38, gt)
        keep = jnp.where(sub == i, 0.0, keep)
        keep = jnp.concatenate([keep, jnp.full((LANES - nbp, blk), NEG_INF, F32)], axis=0)
        qa_ref[g, :, :d] = q
        qa_ref[g, :, d:] = keep.T.astype(BF16)
        qas.append(qa_ref[g])

    def block(g, n):
        kk = ka_ref[g, pl.ds(pl.multiple_of(n * blk, blk), blk), :]
        return lax.dot_general(kk, qas[g], _NT, preferred_element_type=F32) * (scale * LOG2E), vt_ref[g, n]

    carry = []
    for g in range(heads):
        s, vt = block(g, i)
        s = s + bias_ref[g, 0]
        key = lax.broadcasted_iota(I32, s.shape, 0)
        qry = lax.broadcasted_iota(I32, s.shape, 1)
        s = jnp.where(key <= qry, s, NEG_INF)
        m = jnp.max(s, axis=0, keepdims=True)
        p = jnp.exp2(s - m)
        carry.append((m, jnp.sum(p, axis=0, keepdims=True), _dot(vt, p.astype(BF16))))

    def adjacent(n, carry):
        fns = [lambda g=g: block(g, n)[0] + bias_ref[g, 1] for g in range(heads)]
        return _online_softmax_heads(carry, fns, [vt_ref[g, n] for g in range(heads)])

    far_bias = [rb_ref[REL_BUCKETS - 1, hg * heads + g] * LOG2E for g in range(heads)]

    def far(n, carry):
        fns = [lambda g=g: block(g, n)[0] + far_bias[g] for g in range(heads)]
        return _online_softmax_heads(carry, fns, [vt_ref[g, n] for g in range(heads)])

    def far_pair(j, carry):
        def scores(g):
            kk = ka_ref[g, pl.ds(pl.multiple_of(j * (2 * blk), 2 * blk), 2 * blk), :]
            s = lax.dot_general(kk, qas[g], _NT, preferred_element_type=F32)
            return s * (scale * LOG2E) + far_bias[g]
        fns = [lambda g=g: scores(g) for g in range(heads)]
        return _online_softmax_heads(carry, fns, [(vt_ref[g, 2 * j], vt_ref[g, 2 * j + 1]) for g in range(heads)])

    n_far = jnp.maximum(i - 1, 0)
    n_pair = lax.shift_right_logical(n_far, 1)
    carry = lax.fori_loop(n_far, i, adjacent, tuple(carry))
    carry = lax.fori_loop(0, n_pair, far_pair, carry)
    carry = lax.fori_loop(2 * n_pair, n_far, far, carry)
    for g in range(heads):
        m, l, acc = carry[g]
        o_ref[:, g * d:(g + 1) * d] = (acc / l).T.astype(BF16)


def _moba_attn(qkv, et, bias, rel_bias, B, S):
    H, d, blk = MOBA_HEADS, MOBA_HEAD_DIM, MOBA_BLOCK
    T = B * S
    nb = S // blk
    n_sel = min(MOBA_TOPK, nb)
    G = MOBA_HEADS_PER_STEP
    ng = H // G
    return pl.pallas_call(
        functools.partial(_moba_attn_kernel, nb=nb, n_sel=n_sel, scale=d ** -0.5, heads=G),
        grid=(B, ng, nb),
        in_specs=[pl.BlockSpec(memory_space=pltpu.SMEM),
                  pl.BlockSpec((blk, G * d), lambda b, h, i: (b * nb + i, h)),
                  pl.BlockSpec((S, G * d), lambda b, h, i: (b, ng + h)),
                  pl.BlockSpec((S, G * d), lambda b, h, i: (b, 2 * ng + h)),
                  pl.BlockSpec((S, LANES), lambda b, h, i: (0, 0)),
                  pl.BlockSpec((G, 2, blk, blk), lambda b, h, i: (h, 0, 0, 0))],
        out_specs=pl.BlockSpec((blk, G * d), lambda b, h, i: (b * nb + i, h)),
        out_shape=SDS((T, H * d), BF16),
        scratch_shapes=[pltpu.VMEM((G, S, d + LANES), BF16),
                        pltpu.VMEM((G, LANES, d), F32),
                        pltpu.VMEM((G, blk, d + LANES), BF16),
                        pltpu.VMEM((G, nb, d, blk), BF16)],
        compiler_params=_params(3),
        name="moba_attn",
    )(rel_bias, qkv, qkv, qkv, et, bias)


def _out_proj_kernel(ya_ref, yb_ref, g_ref, x_ref, woa_ref, wob_ref, wout_ref, mg_ref, wr_ref, br_ref,
                     h1_ref, xn_ref, lg_ref, *, D):
    a = _dot(ya_ref[...], woa_ref[...])
    b = _dot(yb_ref[...], wob_ref[...])
    g = g_ref[...]
    merged = jax.nn.sigmoid(g[:, :D]) * a + jax.nn.sigmoid(g[:, D:]) * b
    h1 = x_ref[...] + _dot(merged.astype(BF16), wout_ref[...])
    h1_ref[...] = h1
    xn = _rms(h1, mg_ref[...]).astype(BF16)
    lg_ref[...] = _dot(xn, wr_ref[...]) + br_ref[...]
    packed = _pack_bf16_pairs(xn)
    tm, r = packed.shape[0], packed.shape[1] // LANES
    for s in range(r):
        xn_ref[pl.ds(s, tm, stride=r), :] = packed[:, s * LANES:(s + 1) * LANES]


def _pack_bf16_pairs(x):
    bits = lax.bitcast_convert_type(x.astype(F32), U32)
    half = x.shape[1] // 2
    return (bits[:, :half] >> 16) | (bits[:, half:] & U32(HIGH_HALF))


def _unpack_bf16_pairs(w):
    lo = lax.bitcast_convert_type(w << 16, F32).astype(BF16)
    hi = lax.bitcast_convert_type(w & U32(HIGH_HALF), F32).astype(BF16)
    return lo, hi


def _resident(a):
    return pl.BlockSpec(a.shape, lambda i: (0,) * a.ndim, pipeline_mode=pl.Buffered(1))


def _out_proj(ya, yb, gates, x2, woa, wob, wout, mgain, wr, br):
    T, D = x2.shape
    tm = min(OUT_TM, T)
    row = lambda a: pl.BlockSpec((tm, a.shape[1]), lambda i: (i, 0))
    return pl.pallas_call(
        functools.partial(_out_proj_kernel, D=D),
        grid=(T // tm,),
        in_specs=[row(ya), row(yb), row(gates), row(x2),
                  _resident(woa), _resident(wob), _resident(wout), _resident(mgain),
                  _resident(wr), _resident(br)],
        out_specs=[pl.BlockSpec((tm, D), lambda i: (i, 0)),
                   pl.BlockSpec((tm * (D // 2 // LANES), LANES), lambda i: (i, 0)),
                   pl.BlockSpec((tm, LANES), lambda i: (i, 0))],
        out_shape=[SDS((T, D), F32), SDS((T * (D // 2 // LANES), LANES), U32), SDS((T, LANES), F32)],
        compiler_params=_params(1),
        name="out_proj",
    )(ya, yb, gates, x2, woa, wob, wout, mgain, wr, br)


def _route_kernel(lg_ref, e_ref, w_ref, r_ref, cnt_ref, carry_ref, *, n_exp):
    i = pl.program_id(0)

    @pl.when(i == 0)
    def _():
        carry_ref[...] = jnp.zeros(carry_ref.shape, F32)

    lt = lg_ref[...].T
    tm = lt.shape[1]
    sub = lax.broadcasted_iota(I32, lt.shape, 0)
    neg = -jnp.inf
    cur = jnp.where(sub < n_exp, lt, neg)
    vals, idxs = [], []
    for _ in range(TOP_K):
        mx = jnp.max(cur, axis=0, keepdims=True)
        ix = jnp.min(jnp.where(cur == mx, sub, LANES), axis=0, keepdims=True)
        vals.append(mx)
        idxs.append(ix)
        cur = jnp.where(sub == ix, neg, cur)
    ex = [jnp.exp(v - vals[0]) for v in vals]
    den = ex[0]
    for e in ex[1:]:
        den = den + e
    onehot = jnp.zeros(lt.shape, F32)
    for ix in idxs:
        onehot = onehot + (sub == ix).astype(F32)
    r_i = lax.broadcasted_iota(I32, (tm, tm), 0)
    c_i = lax.broadcasted_iota(I32, (tm, tm), 1)
    tri = (r_i <= c_i).astype(BF16)
    incl = _dot(onehot.astype(BF16), tri)
    base = carry_ref[...]
    excl = incl - onehot + base[:, :1]
    e_ref[...] = jnp.zeros(e_ref.shape, I32)
    w_ref[...] = jnp.zeros(w_ref.shape, F32)
    r_ref[...] = jnp.zeros(r_ref.shape, I32)
    for k in range(TOP_K):
        e_ref[k:k + 1, :] = idxs[k]
        w_ref[k:k + 1, :] = ex[k] / den
        rk = jnp.sum(jnp.where(sub == idxs[k], excl, 0.0), axis=0, keepdims=True)
        r_ref[k:k + 1, :] = rk.astype(I32)
    total = base + jnp.sum(onehot, axis=1, keepdims=True)
    carry_ref[...] = total
    cnt_ref[...] = total


def _route(logits, n_exp):
    T = logits.shape[0]
    tm = min(ROUTE_TM, T)
    return pl.pallas_call(
        functools.partial(_route_kernel, n_exp=n_exp),
        grid=(T // tm,),
        in_specs=[pl.BlockSpec((tm, LANES), lambda i: (i, 0))],
        out_specs=[pl.BlockSpec((8, tm), lambda i: (0, i)),
                   pl.BlockSpec((8, tm), lambda i: (0, i)),
                   pl.BlockSpec((8, tm), lambda i: (0, i)),
                   pl.BlockSpec((LANES, LANES), lambda i: (0, 0))],
        out_shape=[SDS((8, T), I32), SDS((8, T), F32), SDS((8, T), I32), SDS((LANES, LANES), F32)],
        scratch_shapes=[pltpu.VMEM((LANES, LANES), F32)],
        compiler_params=_params(1),
        name="route",
    )(logits)


def _dest_kernel(e_ref, r_ref, base_ref, d_ref):
    base = base_ref[...][:, :1]
    sub = lax.broadcasted_iota(I32, (LANES, e_ref.shape[1]), 0)
    d_ref[...] = jnp.zeros(d_ref.shape, I32)
    for k in range(TOP_K):
        off = jnp.sum(jnp.where(sub == e_ref[k:k + 1, :], base, 0), axis=0, keepdims=True)
        d_ref[k:k + 1, :] = r_ref[k:k + 1, :] + off


def _dest(e_k, r_k, base):
    T = e_k.shape[1]
    tb = min(DEST_TB, T)
    blk = pl.BlockSpec((8, tb), lambda i: (0, i))
    return pl.pallas_call(
        _dest_kernel,
        grid=(T // tb,),
        in_specs=[blk, blk, pl.BlockSpec((LANES, LANES), lambda i: (0, 0))],
        out_specs=blk,
        out_shape=SDS((8, T), I32),
        compiler_params=_params(1),
        name="dest",
    )(e_k, r_k, base)


def _dispatch_kernel(zflag_ref, dest_ref, x_ref, xg_hbm, zbuf_ref, sem, *, tm, rows, n_tiles, r):
    i = pl.program_id(0)
    rows = rows * r

    @pl.when(i == 0)
    def _():
        zbuf_ref[...] = jnp.zeros(zbuf_ref.shape, zbuf_ref.dtype)

        def zcopy(j):
            start = pl.multiple_of(j * rows, rows)
            return pltpu.make_async_copy(zbuf_ref, xg_hbm.at[pl.ds(start, rows)], sem)

        def zstart(j, c):
            @pl.when(zflag_ref[j] != 0)
            def _():
                zcopy(j).start()
            return c

        def zwait(j, c):
            @pl.when(zflag_ref[j] != 0)
            def _():
                zcopy(j).wait()
            return c

        lax.fori_loop(0, n_tiles, zstart, 0)
        lax.fori_loop(0, n_tiles, zwait, 0)

    def copy(t, k):
        src = x_ref.at[pl.ds(pl.multiple_of(t * r, r), r)]
        dst = xg_hbm.at[pl.ds(pl.multiple_of(dest_ref[k, t] * r, r), r)]
        return pltpu.make_async_copy(src, dst, sem)

    def start(t, c):
        for k in range(TOP_K):
            copy(t, k).start()
        return c

    def wait(t, c):
        for k in range(TOP_K):
            copy(t, k).wait()
        return c

    lax.fori_loop(0, tm, start, 0, unroll=4)
    lax.fori_loop(0, tm, wait, 0, unroll=4)


def _dispatch(zflag, dest, xn):
    T = dest.shape[1]
    r = xn.shape[0] // T
    tm = min(DISPATCH_TM, T)
    n_chunks = zflag.shape[0]
    return pl.pallas_call(
        functools.partial(_dispatch_kernel, tm=tm, rows=ZERO_ROWS, n_tiles=n_chunks, r=r),
        grid_spec=pltpu.PrefetchScalarGridSpec(
            num_scalar_prefetch=1,
            grid=(T // tm,),
            in_specs=[pl.BlockSpec((8, tm), lambda i, zf: (0, i), memory_space=pltpu.SMEM),
                      pl.BlockSpec((tm * r, LANES), lambda i, zf: (i, 0))],
            out_specs=pl.BlockSpec(memory_space=pl.ANY),
            scratch_shapes=[pltpu.VMEM((ZERO_ROWS * r, LANES), xn.dtype), pltpu.SemaphoreType.DMA]),
        out_shape=SDS((n_chunks * ZERO_ROWS * r, LANES), xn.dtype),
        compiler_params=_params(1),
        name="dispatch",
    )(zflag, dest, xn)


def _row_blocks(valid, n_rows, compute, out_ref):
    full = valid > n_rows - EXPERT_SUB

    @pl.when(full)
    def _():
        compute(slice(0, n_rows))

    for sb in range(n_rows // EXPERT_SUB):
        rows = slice(sb * EXPERT_SUB, (sb + 1) * EXPERT_SUB)

        @pl.when(jnp.logical_not(full) & (sb * EXPERT_SUB < valid))
        def _():
            compute(rows)

        @pl.when(jnp.logical_not(full) & (sb * EXPERT_SUB >= valid))
        def _():
            out_ref[rows, :] = jnp.zeros((EXPERT_SUB, out_ref.shape[1]), out_ref.dtype)


def _ffn_up_kernel(te_ref, tv_ref, nu_ref, x_ref, wg_ref, wu_ref, bg_ref, bu_ref, h_ref, wgb_ref, wub_ref):
    i = pl.program_id(1)
    valid = tv_ref[i]
    new_expert = (i == 0) | (te_ref[i] != te_ref[jnp.maximum(i - 1, 0)])

    @pl.when((valid > 0) & new_expert)
    def _():
        wgb_ref[...] = wg_ref[...].astype(BF16)
        wub_ref[...] = wu_ref[...].astype(BF16)

    r = x_ref.shape[0] // h_ref.shape[0]
    half = r * LANES

    def compute(rows):
        n = rows.stop - rows.start
        parts = [_unpack_bf16_pairs(x_ref[pl.ds(rows.start * r + s, n, stride=r), :]) for s in range(r)]
        lo = jnp.concatenate([lo_s for lo_s, _ in parts], axis=1)
        hi = jnp.concatenate([hi_s for _, hi_s in parts], axis=1)
        g = _dot(lo, wgb_ref[:half, :]) + _dot(hi, wgb_ref[half:, :]) + bg_ref[...]
        u = _dot(lo, wub_ref[:half, :]) + _dot(hi, wub_ref[half:, :]) + bu_ref[...]
        g = jnp.minimum(g, SWIGLU_LIMIT)
        u = jnp.clip(u, -SWIGLU_LIMIT, SWIGLU_LIMIT)
        h_ref[rows, :] = (g * jax.nn.sigmoid(SWIGLU_ALPHA * g) * (u + 1.0)).astype(BF16)

    _row_blocks(valid, h_ref.shape[0], compute, h_ref)


def _ffn_up(tile_expert, tile_valid, n_used, xg, wg, wu, bg, bu):
    D, Dx = wg.shape[1], wg.shape[2]
    r = D // 2 // LANES
    P = xg.shape[0] // r
    tr = EXPERT_ROWS
    tn = min(EXPERT_TN_UP, Dx)
    n_tiles = P // tr
    xmap = lambda j, i, te, tv, nu: (jnp.minimum(i, nu[0] - 1), 0)
    wmap = lambda j, i, te, tv, nu: (te[i], 0, j)
    return pl.pallas_call(
        _ffn_up_kernel,
        grid_spec=pltpu.PrefetchScalarGridSpec(
            num_scalar_prefetch=3,
            grid=(Dx // tn, n_tiles),
            in_specs=[pl.BlockSpec((tr * r, LANES), xmap),
                      pl.BlockSpec((None, D, tn), wmap),
                      pl.BlockSpec((None, D, tn), wmap),
                      pl.BlockSpec((None, 1, tn), wmap),
                      pl.BlockSpec((None, 1, tn), wmap)],
            out_specs=pl.BlockSpec((tr, tn), lambda j, i, te, tv, nu: (i, j)),
            scratch_shapes=[pltpu.VMEM((D, tn), BF16), pltpu.VMEM((D, tn), BF16)]),
        out_shape=SDS((P, Dx), BF16),
        compiler_params=_params(2),
        name="ffn_up",
    )(tile_expert, tile_valid, n_used, xg, wg, wu, bg, bu)


def _ffn_down_kernel(te_ref, tv_ref, nu_ref, h_ref, wd_ref, bd_ref, y_ref, wdb_ref):
    i = pl.program_id(1)
    valid = tv_ref[i]
    new_expert = (i == 0) | (te_ref[i] != te_ref[jnp.maximum(i - 1, 0)])

    @pl.when((valid > 0) & new_expert)
    def _():
        wdb_ref[...] = wd_ref[...].astype(BF16)

    def compute(rows):
        y = _dot(h_ref[rows, :], wdb_ref[...]) + bd_ref[...]
        y_ref[rows, :] = _pack_bf16_pairs(y.astype(BF16))

    _row_blocks(valid, h_ref.shape[0], compute, y_ref)


def _ffn_down(tile_expert, tile_valid, n_used, hid, wd, bd):
    P, Dx = hid.shape
    D = wd.shape[2]
    tr = EXPERT_ROWS
    tn = min(EXPERT_TN_DOWN, D)
    n_tiles = P // tr
    hmap = lambda j, i, te, tv, nu: (jnp.minimum(i, nu[0] - 1), 0)
    wmap = lambda j, i, te, tv, nu: (te[i], 0, j)
    return pl.pallas_call(
        _ffn_down_kernel,
        grid_spec=pltpu.PrefetchScalarGridSpec(
            num_scalar_prefetch=3,
            grid=(D // tn, n_tiles),
            in_specs=[pl.BlockSpec((tr, Dx), hmap),
                      pl.BlockSpec((None, Dx, tn), wmap),
                      pl.BlockSpec((None, 1, tn), wmap)],
            out_specs=pl.BlockSpec((tr, tn // 2), lambda j, i, te, tv, nu: (i, j)),
            scratch_shapes=[pltpu.VMEM((Dx, tn), BF16)]),
        out_shape=SDS((P, D // 2), U32),
        compiler_params=_params(2),
        name="ffn_down",
    )(tile_expert, tile_valid, n_used, hid, wd, bd)


def _combine_kernel(dest_ref, w_ref, h1_ref, p_ref, wple_ref, wpg_ref, pg_ref, fg_ref, y_hbm,
                    o_ref, ybuf_ref, sem, *, tm, pack, final):
    def copy(t, k):
        return pltpu.make_async_copy(y_hbm.at[pl.ds(dest_ref[k, t], 1)],
                                     ybuf_ref.at[k, pl.ds(t, 1)], sem)

    def start(t, c):
        for k in range(TOP_K):
            copy(t, k).start()
        return c

    def wait(t, c):
        for k in range(TOP_K):
            copy(t, k).wait()
        return c

    lax.fori_loop(0, tm, start, 0, unroll=4)
    pw = _dot(p_ref[...].astype(BF16), wple_ref[...])
    lax.fori_loop(0, tm, wait, 0, unroll=4)
    w = w_ref[...]
    parts = []
    for c in range(ybuf_ref.shape[2] // pack):
        cols = slice(c * pack, (c + 1) * pack)
        lo = hi = None
        for k in range(TOP_K):
            words = ybuf_ref[k, :, cols]
            lo_k = w[:, k:k + 1] * lax.bitcast_convert_type(words << 16, F32)
            hi_k = w[:, k:k + 1] * lax.bitcast_convert_type(words & U32(HIGH_HALF), F32)
            lo = lo_k if lo is None else lo + lo_k
            hi = hi_k if hi is None else hi + hi_k
        parts += [lo, hi]
    h2 = h1_ref[...] + jnp.concatenate(parts, axis=1)
    xn = _rms(h2, pg_ref[...]).astype(BF16)
    gate = jax.nn.sigmoid(_dot(xn, wpg_ref[...]))
    h3 = h2 + pw * gate
    o_ref[...] = _rms(h3, fg_ref[...]) if final else h3


def _combine(dest, wts, h1, p2, wple, wpg, pgain, fgain, yg, final):
    T, D = h1.shape
    tm = min(COMBINE_TM, T)
    row = lambda a: pl.BlockSpec((tm, a.shape[1]), lambda i: (i, 0))
    return pl.pallas_call(
        functools.partial(_combine_kernel, tm=tm, pack=min(EXPERT_TN_DOWN, D) // 2, final=final),
        grid=(T // tm,),
        in_specs=[pl.BlockSpec((8, tm), lambda i: (0, i), memory_space=pltpu.SMEM),
                  row(wts), row(h1), row(p2),
                  _resident(wple), _resident(wpg), _resident(pgain), _resident(fgain),
                  pl.BlockSpec(memory_space=pl.ANY)],
        out_specs=pl.BlockSpec((tm, D), lambda i: (i, 0)),
        out_shape=SDS((T, D), F32),
        scratch_shapes=[pltpu.VMEM((TOP_K, tm, D // 2), U32), pltpu.SemaphoreType.DMA],
        compiler_params=_params(1),
        name="combine",
    )(dest, wts, h1, p2, wple, wpg, pgain, fgain, yg)


def kernel(x, p, attn_norm, w_in, q_lat_norm, kv_lat_norm, w_uq, w_ukv, w_o_mla, w_o_moba, w_out,
           rel_bias, moe_norm, w_router, b_router, w_gate, b_gate, w_up, b_up, w_down, b_down,
           ple_norm, w_ple_gate, w_ple, final_norm):
    B, S, D = x.shape
    T = B * S
    n_layers = w_in.shape[0]
    E = w_router.shape[-1]
    H = MLA_HEADS
    mw = MOBA_HEADS * MOBA_HEAD_DIM
    assert S % MOBA_BLOCK == 0 and E <= LANES
    o_kr = MLA_Q_LORA + MLA_KV_LORA
    o_q = o_kr + MLA_ROPE
    o_g = o_q + 3 * mw

    inv = 1.0 / (ROPE_THETA ** (jnp.arange(0, MLA_ROPE, 2, dtype=F32) / MLA_ROPE))
    ang = jnp.arange(S, dtype=F32)[:, None] * inv[None, :]
    cos, sin = jnp.cos(ang), jnp.sin(ang)
    zpad = jnp.zeros((S, LANES - MLA_ROPE), F32)
    cosw = jnp.concatenate([cos, cos, zpad], axis=1)
    sinw = jnp.concatenate([-sin, sin, zpad], axis=1)
    r = jnp.arange(MOBA_BLOCK)
    d0 = r[None, :] - r[:, None]
    bidx = jnp.stack([_t5_bucket(d0), _t5_bucket(d0 + MOBA_BLOCK)]).astype(I32)
    et = (jnp.arange(S)[:, None] // MOBA_BLOCK == jnp.arange(LANES)[None, :]).astype(BF16)
    bias = _moba_bias(rel_bias, bidx)

    h = x.reshape(T, D)
    for li in range(n_layers):
        w = w_in[li]
        w_main = jnp.concatenate([w[:, :o_kr], w[:, o_q:]], axis=1).astype(BF16)
        w_kr = jnp.pad(w[:, o_kr:o_q], ((0, 0), (0, LANES - MLA_ROPE))).astype(BF16)
        wq = w_uq[li].reshape(MLA_Q_LORA, H, MLA_NOPE + MLA_ROPE)
        wq = jnp.pad(wq, ((0, 0), (0, 0), (0, MLA_SLOT - MLA_NOPE - MLA_ROPE)))
        wq = wq.reshape(MLA_Q_LORA, H * MLA_SLOT).astype(BF16)
        wkv = w_ukv[li].reshape(MLA_KV_LORA, H, MLA_NOPE + MLA_V)
        wk = wkv[:, :, :MLA_NOPE].reshape(MLA_KV_LORA, H * MLA_NOPE).astype(BF16)
        wv = wkv[:, :, MLA_NOPE:].reshape(MLA_KV_LORA, H * MLA_V).astype(BF16)
        wr = jnp.pad(w_router[li], ((0, 0), (0, LANES - E))).astype(BF16)
        br = jnp.pad(b_router[li], (0, LANES - E), constant_values=NEG_INF)[None, :]

        lat, qkv, gates, kr = _in_proj(h, attn_norm[li][None, :], w_main, w_kr, o_kr, 3 * mw)
        q_a, k_a, v_a = _mla_proj(lat, kr, cosw, sinw, q_lat_norm[li][None, :], kv_lat_norm[li][None, :],
                                  wq, wk, wv, S)
        y_a = _mla_attn(q_a, k_a, v_a, B, S)
        y_b = _moba_attn(qkv, et, bias, rel_bias, B, S)
        h1, xn, logits = _out_proj(y_a, y_b, gates, h, w_o_mla[li].astype(BF16), w_o_moba[li].astype(BF16),
                                   w_out[li].astype(BF16), moe_norm[li][None, :], wr, br)

        e_k, w_k, r_k, cnt = _route(logits, E)
        counts = cnt[:E, 0].astype(I32)
        tiles = (counts + EXPERT_ROWS - 1) // EXPERT_ROWS
        tile_end = jnp.cumsum(tiles)
        tile_start = tile_end - tiles
        n_tiles = (T * TOP_K) // EXPERT_ROWS + E
        n_used = tile_end[-1]
        tile_ids = jnp.arange(n_tiles)
        capped = jnp.minimum(tile_ids, n_used - 1)
        tile_expert = jnp.minimum(jnp.sum(tile_end[None, :] <= capped[:, None], axis=1), E - 1).astype(I32)
        base = jnp.pad(tile_start * EXPERT_ROWS, (0, LANES - E)).astype(I32)
        dest = _dest(e_k, r_k, jnp.broadcast_to(base[:, None], (LANES, LANES)))
        mine = tile_expert[:, None] == jnp.arange(E)[None, :]
        in_tile = jnp.sum(jnp.where(mine, counts[None, :] - (tile_ids[:, None] - tile_start[None, :]) * EXPERT_ROWS, 0),
                          axis=1)
        tile_valid = jnp.where(tile_ids < n_used, jnp.clip(in_tile, 0, EXPERT_ROWS), 0).astype(I32)
        chunk_lo = jnp.arange(n_tiles * EXPERT_ROWS // ZERO_ROWS) * ZERO_ROWS
        pad_lo = tile_start * EXPERT_ROWS + counts
        pad_hi = tile_end * EXPERT_ROWS
        in_pad = (chunk_lo[:, None] < pad_hi[None, :]) & (chunk_lo[:, None] + ZERO_ROWS > pad_lo[None, :])
        zflag = (jnp.any(in_pad, axis=1) | (chunk_lo >= n_used * EXPERT_ROWS)).astype(I32)
        nu = n_used.astype(I32)[None]

        xg = _dispatch(zflag, dest, xn)
        hid = _ffn_up(tile_expert, tile_valid, nu, xg, w_gate[li], w_up[li],
                      b_gate[li][:, None, :], b_up[li][:, None, :])
        yg = _ffn_down(tile_expert, tile_valid, nu, hid, w_down[li], b_down[li][:, None, :])
        h = _combine(dest, w_k[:TOP_K].T, h1, p[li].reshape(T, -1), w_ple[li].astype(BF16),
                     w_ple_gate[li].astype(BF16), ple_norm[li][None, :], final_norm[None, :], yg,
                     final=li == n_layers - 1)
    return h.reshape(B, S, D)
```

```python
import functools
import math

import jax
import jax.numpy as jnp
from jax import lax
from jax.experimental import pallas as pl
from jax.experimental.pallas import tpu as pltpu

F32 = jnp.float32
BF16 = jnp.bfloat16
I32 = jnp.int32
U32 = jnp.uint32
HIGH_HALF = 0xFFFF0000
SDS = jax.ShapeDtypeStruct

EPS = 1e-6
NEG_INF = -1e30
LOG2E = math.log2(math.e)
MLA_HEADS = 8
MLA_NOPE = 128
MLA_ROPE = 64
MLA_V = 128
MLA_Q_LORA = 512
MLA_KV_LORA = 512
ROPE_THETA = 10000.0
MOBA_HEADS = 8
MOBA_HEAD_DIM = 128
MOBA_BLOCK = 256
MOBA_TOPK = 3
REL_BUCKETS = 32
REL_MAX_DIST = 128
TOP_K = 4
SWIGLU_LIMIT = 7.0
SWIGLU_ALPHA = 1.702

LANES = 128
MLA_SLOT = 2 * LANES
VMEM_LIMIT = 56 * 2**20

IN_TM, IN_TN = 1024, 512
MLAP_TM = 512
MLA_TQ = 512
MLA_TK = 512
MLA_HEADS_PER_STEP = 4
MOBA_HEADS_PER_STEP = 4
OUT_TM = 256
ROUTE_TM = 512
EXPERT_ROWS = 1024
EXPERT_SUB = 256
EXPERT_TN_UP = 512
EXPERT_TN_DOWN = 1024
ZERO_ROWS = 512
DISPATCH_TM = 512
DEST_TB = 2048
COMBINE_TM = 512

_NT = (((1,), (1,)), ((), ()))


def _params(n_axes):
    return pltpu.CompilerParams(dimension_semantics=("arbitrary",) * n_axes,
                                vmem_limit_bytes=VMEM_LIMIT)


def _rms(x, g):
    return x * lax.rsqrt(jnp.mean(x * x, axis=-1, keepdims=True) + EPS) * g


def _dot(a, b):
    return jnp.dot(a, b, preferred_element_type=F32)


def _in_proj_kernel(x_ref, g_ref, w_ref, wkr_ref, lat_ref, qkv_ref, gates_ref, kr_ref, xn_ref,
                    *, n_lat, n_qkv):
    j = pl.program_id(1)

    @pl.when(j == 0)
    def _():
        xn = _rms(x_ref[...], g_ref[...]).astype(BF16)
        xn_ref[...] = xn
        kr_ref[...] = _dot(xn, wkr_ref[...])

    @pl.when(j < n_lat)
    def _():
        lat_ref[...] = _dot(xn_ref[...], w_ref[...])

    @pl.when((j >= n_lat) & (j < n_lat + n_qkv))
    def _():
        qkv_ref[...] = _dot(xn_ref[...], w_ref[...]).astype(BF16)

    @pl.when(j >= n_lat + n_qkv)
    def _():
        gates_ref[...] = _dot(xn_ref[...], w_ref[...])


def _in_proj(x2, gain, w_main, w_kr, n_lat_cols, n_qkv_cols):
    T, D = x2.shape
    n_g_cols = w_main.shape[1] - n_lat_cols - n_qkv_cols
    tm = min(IN_TM, T)
    tn = math.gcd(math.gcd(IN_TN, n_lat_cols), math.gcd(n_qkv_cols, n_g_cols))
    n_lat, n_qkv, n_g = n_lat_cols // tn, n_qkv_cols // tn, n_g_cols // tn
    return pl.pallas_call(
        functools.partial(_in_proj_kernel, n_lat=n_lat, n_qkv=n_qkv),
        grid=(T // tm, n_lat + n_qkv + n_g),
        in_specs=[pl.BlockSpec((tm, D), lambda i, j: (i, 0)),
                  pl.BlockSpec((1, D), lambda i, j: (0, 0)),
                  pl.BlockSpec((D, tn), lambda i, j: (0, j)),
                  pl.BlockSpec((D, LANES), lambda i, j: (0, 0))],
        out_specs=[pl.BlockSpec((tm, tn), lambda i, j: (i, jnp.minimum(j, n_lat - 1))),
                   pl.BlockSpec((tm, tn), lambda i, j: (i, jnp.clip(j - n_lat, 0, n_qkv - 1))),
                   pl.BlockSpec((tm, tn), lambda i, j: (i, jnp.maximum(j - n_lat - n_qkv, 0))),
                   pl.BlockSpec((tm, LANES), lambda i, j: (i, 0))],
        out_shape=[SDS((T, n_lat_cols), F32), SDS((T, n_qkv_cols), BF16),
                   SDS((T, n_g_cols), F32), SDS((T, LANES), F32)],
        scratch_shapes=[pltpu.VMEM((tm, D), BF16)],
        compiler_params=_params(2),
        name="in_proj",
    )(x2, gain, w_main, w_kr)


def _mla_proj_kernel(lat_ref, kr_ref, cos_ref, sin_ref, qn_ref, kvn_ref, wq_ref, wk_ref, wv_ref,
                     q_ref, k_ref, v_ref):
    lat = lat_ref[...]
    qn = _rms(lat[:, :MLA_Q_LORA], qn_ref[...]).astype(BF16)
    kvn = _rms(lat[:, MLA_Q_LORA:], kvn_ref[...]).astype(BF16)
    q = _dot(qn, wq_ref[...])
    kn = _dot(kvn, wk_ref[...])
    v_ref[...] = _dot(kvn, wv_ref[...]).astype(BF16)
    c = cos_ref[...]
    s = sin_ref[...]
    half = MLA_ROPE // 2
    lane = lax.broadcasted_iota(I32, c.shape, 1)

    def rope(xr):
        swapped = jnp.where(lane < half, pltpu.roll(xr, LANES - half, 1), pltpu.roll(xr, half, 1))
        return xr * c + swapped * s

    kr = rope(kr_ref[...]).astype(BF16)
    for h in range(MLA_HEADS):
        lo = h * MLA_SLOT
        q_ref[:, lo:lo + LANES] = q[:, lo:lo + LANES].astype(BF16)
        q_ref[:, lo + LANES:lo + MLA_SLOT] = rope(q[:, lo + LANES:lo + MLA_SLOT]).astype(BF16)
        k_ref[:, lo:lo + LANES] = kn[:, h * MLA_NOPE:(h + 1) * MLA_NOPE].astype(BF16)
        k_ref[:, lo + LANES:lo + MLA_SLOT] = kr


def _mla_proj(lat, kr, cosw, sinw, qnorm, kvnorm, wq, wk, wv, S):
    T = lat.shape[0]
    tm = min(MLAP_TM, S)
    ns = S // tm
    H = MLA_HEADS
    full = lambda a: pl.BlockSpec(a.shape, lambda i: (0,) * a.ndim)
    return pl.pallas_call(
        _mla_proj_kernel,
        grid=(T // tm,),
        in_specs=[pl.BlockSpec((tm, lat.shape[1]), lambda i: (i, 0)),
                  pl.BlockSpec((tm, LANES), lambda i: (i, 0)),
                  pl.BlockSpec((tm, LANES), lambda i: (i % ns, 0)),
                  pl.BlockSpec((tm, LANES), lambda i: (i % ns, 0)),
                  full(qnorm), full(kvnorm), full(wq), full(wk), full(wv)],
        out_specs=[pl.BlockSpec((tm, H * MLA_SLOT), lambda i: (i, 0)),
                   pl.BlockSpec((tm, H * MLA_SLOT), lambda i: (i, 0)),
                   pl.BlockSpec((tm, H * MLA_V), lambda i: (i, 0))],
        out_shape=[SDS((T, H * MLA_SLOT), BF16), SDS((T, H * MLA_SLOT), BF16),
                   SDS((T, H * MLA_V), BF16)],
        compiler_params=_params(1),
        name="mla_proj",
    )(lat, kr, cosw, sinw, qnorm, kvnorm, wq, wk, wv)


def _transpose_chunks(src_ref, dst_ref):
    n, _, tk = dst_ref.shape
    for c in range(n):
        dst_ref[c] = src_ref[c * tk:(c + 1) * tk, :].astype(F32).T.astype(dst_ref.dtype)


def _online_softmax_t(carry, s, vt):
    return _online_softmax_heads((carry,), (lambda: s,), (vt,))[0]


def _online_softmax_heads(carries, score_fns, vts):
    n = len(carries)
    scores, stats, out = {}, {}, [None] * n
    for step in range(n + 2):
        if step < n:
            scores[step] = score_fns[step]()
        g = step - 1
        if 0 <= g < n:
            m, l, _ = carries[g]
            s = scores.pop(g)
            m_new = jnp.maximum(m, jnp.max(s, axis=0, keepdims=True))
            alpha = jnp.exp2(m - m_new)
            p = jnp.exp2(s - m_new)
            stats[g] = (m_new, alpha * l + jnp.sum(p, axis=0, keepdims=True), alpha, p.astype(BF16))
        g = step - 2
        if 0 <= g < n:
            m_new, l, alpha, p = stats.pop(g)
            chunks = vts[g] if isinstance(vts[g], (tuple, list)) else (vts[g],)
            rows = p.shape[0] // len(chunks)
            pv = _dot(chunks[0], p[:rows])
            for c in range(1, len(chunks)):
                pv = pv + _dot(chunks[c], p[c * rows:(c + 1) * rows])
            out[g] = (m_new, l, alpha * carries[g][2] + pv)
    return tuple(out)


def _mla_attn_kernel(q_ref, k_ref, v_ref, o_ref, vt_ref, *, tq, tk, scale, heads):
    qi = pl.program_id(2)
    slot, dv = MLA_SLOT, MLA_V

    @pl.when(qi == 0)
    def _():
        for g in range(heads):
            _transpose_chunks(v_ref.at[:, g * dv:(g + 1) * dv], vt_ref.at[g])

    qs = [q_ref[:, g * slot:(g + 1) * slot] for g in range(heads)]
    per_q = tq // tk

    def chunk(g, c):
        k = k_ref[pl.ds(pl.multiple_of(c * tk, tk), tk), g * slot:(g + 1) * slot]
        return lax.dot_general(k, qs[g], _NT, preferred_element_type=F32) * (scale * LOG2E), vt_ref[g, c]

    carry = [None] * heads
    for c in range(per_q):
        for g in range(heads):
            s, vt = chunk(g, qi * per_q + c)
            key = lax.broadcasted_iota(I32, s.shape, 0) + c * tk
            qry = lax.broadcasted_iota(I32, s.shape, 1)
            s = jnp.where(key <= qry, s, NEG_INF)
            if carry[g] is None:
                m = jnp.max(s, axis=0, keepdims=True)
                p = jnp.exp2(s - m)
                carry[g] = (m, jnp.sum(p, axis=0, keepdims=True), _dot(vt, p.astype(BF16)))
            else:
                carry[g] = _online_softmax_t(carry[g], s, vt)

    def body(c, carry):
        fns = [lambda g=g: chunk(g, c)[0] for g in range(heads)]
        return _online_softmax_heads(carry, fns, [vt_ref[g, c] for g in range(heads)])

    carry = lax.fori_loop(0, qi * per_q, body, tuple(carry))
    for g in range(heads):
        m, l, acc = carry[g]
        o_ref[:, g * dv:(g + 1) * dv] = (acc / l).T.astype(BF16)


def _mla_attn(q, k, v, B, S):
    H = MLA_HEADS
    G = MLA_HEADS_PER_STEP
    T = B * S
    tq = min(MLA_TQ, S)
    tk = min(MLA_TK, tq)
    nq = S // tq
    scale = (MLA_NOPE + MLA_ROPE) ** -0.5
    return pl.pallas_call(
        functools.partial(_mla_attn_kernel, tq=tq, tk=tk, scale=scale, heads=G),
        grid=(B, H // G, nq),
        in_specs=[pl.BlockSpec((tq, G * MLA_SLOT), lambda b, h, i: (b * nq + i, h)),
                  pl.BlockSpec((S, G * MLA_SLOT), lambda b, h, i: (b, h)),
                  pl.BlockSpec((S, G * MLA_V), lambda b, h, i: (b, h))],
        out_specs=pl.BlockSpec((tq, G * MLA_V), lambda b, h, i: (b * nq + i, h)),
        out_shape=SDS((T, H * MLA_V), BF16),
        scratch_shapes=[pltpu.VMEM((G, S // tk, MLA_V, tk), BF16)],
        compiler_params=_params(3),
        name="mla_attn",
    )(q, k, v)


def _t5_bucket(dist):
    n = jnp.maximum(dist, 0)
    max_exact = REL_BUCKETS // 2
    large = max_exact + (jnp.log(jnp.maximum(n, 1).astype(F32) / max_exact)
                         / math.log(REL_MAX_DIST / max_exact)
                         * (REL_BUCKETS - max_exact)).astype(I32)
    large = jnp.minimum(large, REL_BUCKETS - 1)
    return jnp.where(n < max_exact, n, large)


def _moba_bias_kernel(rb_ref, bidx_ref, o_ref):
    h = pl.program_id(0)
    for t in range(2):
        bi = bidx_ref[t]
        val = jnp.zeros(bi.shape, F32)
        for b in range(REL_BUCKETS):
            val = jnp.where(bi == b, rb_ref[b, h], val)
        o_ref[t] = val * LOG2E


def _moba_bias(rel_bias, bidx):
    H = rel_bias.shape[1]
    blk = MOBA_BLOCK
    return pl.pallas_call(
        _moba_bias_kernel,
        grid=(H,),
        in_specs=[pl.BlockSpec(memory_space=pltpu.SMEM),
                  pl.BlockSpec((2, blk, blk), lambda h: (0, 0, 0))],
        out_specs=pl.BlockSpec((None, 2, blk, blk), lambda h: (h, 0, 0, 0)),
        out_shape=SDS((H, 2, blk, blk), F32),
        compiler_params=_params(1),
        name="moba_bias",
    )(rel_bias, bidx)


def _moba_attn_kernel(rb_ref, q_ref, k_ref, v_ref, et_ref, bias_ref, o_ref, ka_ref, km_ref, qa_ref, vt_ref,
                      *, nb, n_sel, scale, heads):
    hg = pl.program_id(1)
    i = pl.program_id(2)
    blk = MOBA_BLOCK
    d = MOBA_HEAD_DIM
    nbp = -(-nb // 8) * 8

    @pl.when(i == 0)
    def _():
        for g in range(heads):
            cols = slice(g * d, (g + 1) * d)
            ka_ref[g, :, :d] = k_ref[:, cols]
            ka_ref[g, :, d:] = et_ref[...]
            _transpose_chunks(v_ref.at[:, cols], vt_ref.at[g])
            km_ref[g] = jnp.zeros(km_ref.shape[1:], F32)
            for n in range(nb):
                kb = k_ref[n * blk:(n + 1) * blk, cols].astype(F32)
                km_ref[g, n:n + 1, :] = jnp.sum(kb, axis=0, keepdims=True) * (1.0 / blk)

    qas = []
    for g in range(heads):
        q = q_ref[:, g * d:(g + 1) * d]
        gate = lax.dot_general(km_ref[g, :nbp, :].astype(BF16), q, _NT, preferred_element_type=F32)
        sub = lax.broadcasted_iota(I32, gate.shape, 0)
        gt = jnp.where(sub < i, gate, NEG_INF)
        keep = jnp.full(gate.shape, NEG_INF, F32)
        for _ in range(n_sel):
            mx = jnp.max(gt, axis=0, keepdims=True)
            first = jnp.min(jnp.where(gt == mx, sub, LANES), axis=0, keepdims=True)
            pick = sub == first
            keep = jnp.where(pick & (sub < i), 0.0, keep)
            gt = jnp.where(pick, -3.0e38, gt)
        keep = jnp.where(sub == i, 0.0, keep)
        keep = jnp.concatenate([keep, jnp.full((LANES - nbp, blk), NEG_INF, F32)], axis=0)
        qa_ref[g, :, :d] = q
        qa_ref[g, :, d:] = keep.T.astype(BF16)
        qas.append(qa_ref[g])

    def block(g, n):
        kk = ka_ref[g, pl.ds(pl.multiple_of(n * blk, blk), blk), :]
        return lax.dot_general(kk, qas[g], _NT, preferred_element_type=F32) * (scale * LOG2E), vt_ref[g, n]

    carry = []
    for g in range(heads):
        s, vt = block(g, i)
        s = s + bias_ref[g, 0]
        key = lax.broadcasted_iota(I32, s.shape, 0)
        qry = lax.broadcasted_iota(I32, s.shape, 1)
        s = jnp.where(key <= qry, s, NEG_INF)
        m = jnp.max(s, axis=0, keepdims=True)
        p = jnp.exp2(s - m)
        carry.append((m, jnp.sum(p, axis=0, keepdims=True), _dot(vt, p.astype(BF16))))

    def adjacent(n, carry):
        fns = [lambda g=g: block(g, n)[0] + bias_ref[g, 1] for g in range(heads)]
        return _online_softmax_heads(carry, fns, [vt_ref[g, n] for g in range(heads)])

    far_bias = [rb_ref[REL_BUCKETS - 1, hg * heads + g] * LOG2E for g in range(heads)]

    def far(n, carry):
        fns = [lambda g=g: block(g, n)[0] + far_bias[g] for g in range(heads)]
        return _online_softmax_heads(carry, fns, [vt_ref[g, n] for g in range(heads)])

    def far_pair(j, carry):
        def scores(g):
            kk = ka_ref[g, pl.ds(pl.multiple_of(j * (2 * blk), 2 * blk), 2 * blk), :]
            s = lax.dot_general(kk, qas[g], _NT, preferred_element_type=F32)
            return s * (scale * LOG2E) + far_bias[g]
        fns = [lambda g=g: scores(g) for g in range(heads)]
        return _online_softmax_heads(carry, fns, [(vt_ref[g, 2 * j], vt_ref[g, 2 * j + 1]) for g in range(heads)])

    n_far = jnp.maximum(i - 1, 0)
    n_pair = lax.shift_right_logical(n_far, 1)
    carry = lax.fori_loop(n_far, i, adjacent, tuple(carry))
    carry = lax.fori_loop(0, n_pair, far_pair, carry)
    carry = lax.fori_loop(2 * n_pair, n_far, far, carry)
    for g in range(heads):
        m, l, acc = carry[g]
        o_ref[:, g * d:(g + 1) * d] = (acc / l).T.astype(BF16)


def _moba_attn(qkv, et, bias, rel_bias, B, S):
    H, d, blk = MOBA_HEADS, MOBA_HEAD_DIM, MOBA_BLOCK
    T = B * S
    nb = S // blk
    n_sel = min(MOBA_TOPK, nb)
    G = MOBA_HEADS_PER_STEP
    ng = H // G
    return pl.pallas_call(
        functools.partial(_moba_attn_kernel, nb=nb, n_sel=n_sel, scale=d ** -0.5, heads=G),
        grid=(B, ng, nb),
        in_specs=[pl.BlockSpec(memory_space=pltpu.SMEM),
                  pl.BlockSpec((blk, G * d), lambda b, h, i: (b * nb + i, h)),
                  pl.BlockSpec((S, G * d), lambda b, h, i: (b, ng + h)),
                  pl.BlockSpec((S, G * d), lambda b, h, i: (b, 2 * ng + h)),
                  pl.BlockSpec((S, LANES), lambda b, h, i: (0, 0)),
                  pl.BlockSpec((G, 2, blk, blk), lambda b, h, i: (h, 0, 0, 0))],
        out_specs=pl.BlockSpec((blk, G * d), lambda b, h, i: (b * nb + i, h)),
        out_shape=SDS((T, H * d), BF16),
        scratch_shapes=[pltpu.VMEM((G, S, d + LANES), BF16),
                        pltpu.VMEM((G, LANES, d), F32),
                        pltpu.VMEM((G, blk, d + LANES), BF16),
                        pltpu.VMEM((G, nb, d, blk), BF16)],
        compiler_params=_params(3),
        name="moba_attn",
    )(rel_bias, qkv, qkv, qkv, et, bias)


def _out_proj_kernel(ya_ref, yb_ref, g_ref, x_ref, woa_ref, wob_ref, wout_ref, mg_ref, wr_ref, br_ref,
                     h1_ref, xn_ref, lg_ref, *, D):
    a = _dot(ya_ref[...], woa_ref[...])
    b = _dot(yb_ref[...], wob_ref[...])
    g = g_ref[...]
    merged = jax.nn.sigmoid(g[:, :D]) * a + jax.nn.sigmoid(g[:, D:]) * b
    h1 = x_ref[...] + _dot(merged.astype(BF16), wout_ref[...])
    h1_ref[...] = h1
    xn = _rms(h1, mg_ref[...]).astype(BF16)
    lg_ref[...] = _dot(xn, wr_ref[...]) + br_ref[...]
    packed = _pack_bf16_pairs(xn)
    tm, r = packed.shape[0], packed.shape[1] // LANES
    for s in range(r):
        xn_ref[pl.ds(s, tm, stride=r), :] = packed[:, s * LANES:(s + 1) * LANES]


def _pack_bf16_pairs(x):
    bits = lax.bitcast_convert_type(x.astype(F32), U32)
    half = x.shape[1] // 2
    return (bits[:, :half] >> 16) | (bits[:, half:] & U32(HIGH_HALF))


def _unpack_bf16_pairs(w):
    lo = lax.bitcast_convert_type(w << 16, F32).astype(BF16)
    hi = lax.bitcast_convert_type(w & U32(HIGH_HALF), F32).astype(BF16)
    return lo, hi


def _resident(a):
    return pl.BlockSpec(a.shape, lambda i: (0,) * a.ndim, pipeline_mode=pl.Buffered(1))


def _out_proj(ya, yb, gates, x2, woa, wob, wout, mgain, wr, br):
    T, D = x2.shape
    tm = min(OUT_TM, T)
    row = lambda a: pl.BlockSpec((tm, a.shape[1]), lambda i: (i, 0))
    return pl.pallas_call(
        functools.partial(_out_proj_kernel, D=D),
        grid=(T // tm,),
        in_specs=[row(ya), row(yb), row(gates), row(x2),
                  _resident(woa), _resident(wob), _resident(wout), _resident(mgain),
                  _resident(wr), _resident(br)],
        out_specs=[pl.BlockSpec((tm, D), lambda i: (i, 0)),
                   pl.BlockSpec((tm * (D // 2 // LANES), LANES), lambda i: (i, 0)),
                   pl.BlockSpec((tm, LANES), lambda i: (i, 0))],
        out_shape=[SDS((T, D), F32), SDS((T * (D // 2 // LANES), LANES), U32), SDS((T, LANES), F32)],
        compiler_params=_params(1),
        name="out_proj",
    )(ya, yb, gates, x2, woa, wob, wout, mgain, wr, br)


def _route_kernel(lg_ref, e_ref, w_ref, r_ref, cnt_ref, carry_ref, *, n_exp):
    i = pl.program_id(0)

    @pl.when(i == 0)
    def _():
        carry_ref[...] = jnp.zeros(carry_ref.shape, F32)

    lt = lg_ref[...].T
    tm = lt.shape[1]
    sub = lax.broadcasted_iota(I32, lt.shape, 0)
    neg = -jnp.inf
    cur = jnp.where(sub < n_exp, lt, neg)
    vals, idxs = [], []
    for _ in range(TOP_K):
        mx = jnp.max(cur, axis=0, keepdims=True)
        ix = jnp.min(jnp.where(cur == mx, sub, LANES), axis=0, keepdims=True)
        vals.append(mx)
        idxs.append(ix)
        cur = jnp.where(sub == ix, neg, cur)
    ex = [jnp.exp(v - vals[0]) for v in vals]
    den = ex[0]
    for e in ex[1:]:
        den = den + e
    onehot = jnp.zeros(lt.shape, F32)
    for ix in idxs:
        onehot = onehot + (sub == ix).astype(F32)
    r_i = lax.broadcasted_iota(I32, (tm, tm), 0)
    c_i = lax.broadcasted_iota(I32, (tm, tm), 1)
    tri = (r_i <= c_i).astype(BF16)
    incl = _dot(onehot.astype(BF16), tri)
    base = carry_ref[...]
    excl = incl - onehot + base[:, :1]
    e_ref[...] = jnp.zeros(e_ref.shape, I32)
    w_ref[...] = jnp.zeros(w_ref.shape, F32)
    r_ref[...] = jnp.zeros(r_ref.shape, I32)
    for k in range(TOP_K):
        e_ref[k:k + 1, :] = idxs[k]
        w_ref[k:k + 1, :] = ex[k] / den
        rk = jnp.sum(jnp.where(sub == idxs[k], excl, 0.0), axis=0, keepdims=True)
        r_ref[k:k + 1, :] = rk.astype(I32)
    total = base + jnp.sum(onehot, axis=1, keepdims=True)
    carry_ref[...] = total
    cnt_ref[...] = total


def _route(logits, n_exp):
    T = logits.shape[0]
    tm = min(ROUTE_TM, T)
    return pl.pallas_call(
        functools.partial(_route_kernel, n_exp=n_exp),
        grid=(T // tm,),
        in_specs=[pl.BlockSpec((tm, LANES), lambda i: (i, 0))],
        out_specs=[pl.BlockSpec((8, tm), lambda i: (0, i)),
                   pl.BlockSpec((8, tm), lambda i: (0, i)),
                   pl.BlockSpec((8, tm), lambda i: (0, i)),
                   pl.BlockSpec((LANES, LANES), lambda i: (0, 0))],
        out_shape=[SDS((8, T), I32), SDS((8, T), F32), SDS((8, T), I32), SDS((LANES, LANES), F32)],
        scratch_shapes=[pltpu.VMEM((LANES, LANES), F32)],
        compiler_params=_params(1),
        name="route",
    )(logits)


def _dest_kernel(e_ref, r_ref, base_ref, d_ref):
    base = base_ref[...][:, :1]
    sub = lax.broadcasted_iota(I32, (LANES, e_ref.shape[1]), 0)
    d_ref[...] = jnp.zeros(d_ref.shape, I32)
    for k in range(TOP_K):
        off = jnp.sum(jnp.where(sub == e_ref[k:k + 1, :], base, 0), axis=0, keepdims=True)
        d_ref[k:k + 1, :] = r_ref[k:k + 1, :] + off


def _dest(e_k, r_k, base):
    T = e_k.shape[1]
    tb = min(DEST_TB, T)
    blk = pl.BlockSpec((8, tb), lambda i: (0, i))
    return pl.pallas_call(
        _dest_kernel,
        grid=(T // tb,),
        in_specs=[blk, blk, pl.BlockSpec((LANES, LANES), lambda i: (0, 0))],
        out_specs=blk,
        out_shape=SDS((8, T), I32),
        compiler_params=_params(1),
        name="dest",
    )(e_k, r_k, base)


def _dispatch_kernel(zflag_ref, dest_ref, x_ref, xg_hbm, zbuf_ref, sem, *, tm, rows, n_tiles, r):
    i = pl.program_id(0)
    rows = rows * r

    @pl.when(i == 0)
    def _():
        zbuf_ref[...] = jnp.zeros(zbuf_ref.shape, zbuf_ref.dtype)

        def zcopy(j):
            start = pl.multiple_of(j * rows, rows)
            return pltpu.make_async_copy(zbuf_ref, xg_hbm.at[pl.ds(start, rows)], sem)

        def zstart(j, c):
            @pl.when(zflag_ref[j] != 0)
            def _():
                zcopy(j).start()
            return c

        def zwait(j, c):
            @pl.when(zflag_ref[j] != 0)
            def _():
                zcopy(j).wait()
            return c

        lax.fori_loop(0, n_tiles, zstart, 0)
        lax.fori_loop(0, n_tiles, zwait, 0)

    def copy(t, k):
        src = x_ref.at[pl.ds(pl.multiple_of(t * r, r), r)]
        dst = xg_hbm.at[pl.ds(pl.multiple_of(dest_ref[k, t] * r, r), r)]
        return pltpu.make_async_copy(src, dst, sem)

    def start(t, c):
        for k in range(TOP_K):
            copy(t, k).start(priority=k % 2)
        return c

    def wait(t, c):
        for k in range(TOP_K):
            copy(t, k).wait()
        return c

    lax.fori_loop(0, tm, start, 0, unroll=4)
    lax.fori_loop(0, tm, wait, 0, unroll=4)


def _dispatch(zflag, dest, xn):
    T = dest.shape[1]
    r = xn.shape[0] // T
    tm = min(DISPATCH_TM, T)
    n_chunks = zflag.shape[0]
    return pl.pallas_call(
        functools.partial(_dispatch_kernel, tm=tm, rows=ZERO_ROWS, n_tiles=n_chunks, r=r),
        grid_spec=pltpu.PrefetchScalarGridSpec(
            num_scalar_prefetch=1,
            grid=(T // tm,),
            in_specs=[pl.BlockSpec((8, tm), lambda i, zf: (0, i), memory_space=pltpu.SMEM),
                      pl.BlockSpec((tm * r, LANES), lambda i, zf: (i, 0))],
            out_specs=pl.BlockSpec(memory_space=pl.ANY),
            scratch_shapes=[pltpu.VMEM((ZERO_ROWS * r, LANES), xn.dtype), pltpu.SemaphoreType.DMA]),
        out_shape=SDS((n_chunks * ZERO_ROWS * r, LANES), xn.dtype),
        compiler_params=_params(1),
        name="dispatch",
    )(zflag, dest, xn)


def _row_blocks(valid, n_rows, compute, out_ref):
    full = valid > n_rows - EXPERT_SUB

    @pl.when(full)
    def _():
        compute(slice(0, n_rows))

    for sb in range(n_rows // EXPERT_SUB):
        rows = slice(sb * EXPERT_SUB, (sb + 1) * EXPERT_SUB)

        @pl.when(jnp.logical_not(full) & (sb * EXPERT_SUB < valid))
        def _():
            compute(rows)

        @pl.when(jnp.logical_not(full) & (sb * EXPERT_SUB >= valid))
        def _():
            out_ref[rows, :] = jnp.zeros((EXPERT_SUB, out_ref.shape[1]), out_ref.dtype)


def _ffn_up_kernel(te_ref, tv_ref, nu_ref, x_ref, wg_ref, wu_ref, bg_ref, bu_ref, h_ref, wgb_ref, wub_ref):
    i = pl.program_id(1)
    valid = tv_ref[i]
    new_expert = (i == 0) | (te_ref[i] != te_ref[jnp.maximum(i - 1, 0)])

    @pl.when((valid > 0) & new_expert)
    def _():
        wgb_ref[...] = wg_ref[...].astype(BF16)
        wub_ref[...] = wu_ref[...].astype(BF16)

    r = x_ref.shape[0] // h_ref.shape[0]
    half = r * LANES

    def compute(rows):
        n = rows.stop - rows.start
        parts = [_unpack_bf16_pairs(x_ref[pl.ds(rows.start * r + s, n, stride=r), :]) for s in range(r)]
        lo = jnp.concatenate([lo_s for lo_s, _ in parts], axis=1)
        hi = jnp.concatenate([hi_s for _, hi_s in parts], axis=1)
        g = _dot(lo, wgb_ref[:half, :]) + _dot(hi, wgb_ref[half:, :]) + bg_ref[...]
        u = _dot(lo, wub_ref[:half, :]) + _dot(hi, wub_ref[half:, :]) + bu_ref[...]
        g = jnp.minimum(g, SWIGLU_LIMIT)
        u = jnp.clip(u, -SWIGLU_LIMIT, SWIGLU_LIMIT)
        h_ref[rows, :] = (g * jax.nn.sigmoid(SWIGLU_ALPHA * g) * (u + 1.0)).astype(BF16)

    _row_blocks(valid, h_ref.shape[0], compute, h_ref)


def _ffn_up(tile_expert, tile_valid, n_used, xg, wg, wu, bg, bu):
    D, Dx = wg.shape[1], wg.shape[2]
    r = D // 2 // LANES
    P = xg.shape[0] // r
    tr = EXPERT_ROWS
    tn = min(EXPERT_TN_UP, Dx)
    n_tiles = P // tr
    xmap = lambda j, i, te, tv, nu: (jnp.minimum(i, nu[0] - 1), 0)
    wmap = lambda j, i, te, tv, nu: (te[i], 0, j)
    return pl.pallas_call(
        _ffn_up_kernel,
        grid_spec=pltpu.PrefetchScalarGridSpec(
            num_scalar_prefetch=3,
            grid=(Dx // tn, n_tiles),
            in_specs=[pl.BlockSpec((tr * r, LANES), xmap),
                      pl.BlockSpec((None, D, tn), wmap),
                      pl.BlockSpec((None, D, tn), wmap),
                      pl.BlockSpec((None, 1, tn), wmap),
                      pl.BlockSpec((None, 1, tn), wmap)],
            out_specs=pl.BlockSpec((tr, tn), lambda j, i, te, tv, nu: (i, j)),
            scratch_shapes=[pltpu.VMEM((D, tn), BF16), pltpu.VMEM((D, tn), BF16)]),
        out_shape=SDS((P, Dx), BF16),
        compiler_params=_params(2),
        name="ffn_up",
    )(tile_expert, tile_valid, n_used, xg, wg, wu, bg, bu)


def _ffn_down_kernel(te_ref, tv_ref, nu_ref, h_ref, wd_ref, bd_ref, y_ref, wdb_ref):
    i = pl.program_id(1)
    valid = tv_ref[i]
    new_expert = (i == 0) | (te_ref[i] != te_ref[jnp.maximum(i - 1, 0)])

    @pl.when((valid > 0) & new_expert)
    def _():
        wdb_ref[...] = wd_ref[...].astype(BF16)

    def compute(rows):
        y = _dot(h_ref[rows, :], wdb_ref[...]) + bd_ref[...]
        y_ref[rows, :] = _pack_bf16_pairs(y.astype(BF16))

    _row_blocks(valid, h_ref.shape[0], compute, y_ref)


def _ffn_down(tile_expert, tile_valid, n_used, hid, wd, bd):
    P, Dx = hid.shape
    D = wd.shape[2]
    tr = EXPERT_ROWS
    tn = min(EXPERT_TN_DOWN, D)
    n_tiles = P // tr
    hmap = lambda j, i, te, tv, nu: (jnp.minimum(i, nu[0] - 1), 0)
    wmap = lambda j, i, te, tv, nu: (te[i], 0, j)
    return pl.pallas_call(
        _ffn_down_kernel,
        grid_spec=pltpu.PrefetchScalarGridSpec(
            num_scalar_prefetch=3,
            grid=(D // tn, n_tiles),
            in_specs=[pl.BlockSpec((tr, Dx), hmap),
                      pl.BlockSpec((None, Dx, tn), wmap),
                      pl.BlockSpec((None, 1, tn), wmap)],
            out_specs=pl.BlockSpec((tr, tn // 2), lambda j, i, te, tv, nu: (i, j)),
            scratch_shapes=[pltpu.VMEM((Dx, tn), BF16)]),
        out_shape=SDS((P, D // 2), U32),
        compiler_params=_params(2),
        name="ffn_down",
    )(tile_expert, tile_valid, n_used, hid, wd, bd)


def _combine_kernel(dest_ref, w_ref, h1_ref, p_ref, wple_ref, wpg_ref, pg_ref, fg_ref, y_hbm,
                    o_ref, ybuf_ref, sem, *, tm, pack, final):
    def copy(t, k):
        return pltpu.make_async_copy(y_hbm.at[pl.ds(dest_ref[k, t], 1)],
                                     ybuf_ref.at[k, pl.ds(t, 1)], sem)

    def start(t, c):
        for k in range(TOP_K):
            copy(t, k).start(priority=k % 2)
        return c

    def wait(t, c):
        for k in range(TOP_K):
            copy(t, k).wait()
        return c

    lax.fori_loop(0, tm, start, 0, unroll=4)
    pw = _dot(p_ref[...].astype(BF16), wple_ref[...])
    lax.fori_loop(0, tm, wait, 0, unroll=4)
    w = w_ref[...]
    parts = []
    for c in range(ybuf_ref.shape[2] // pack):
        cols = slice(c * pack, (c + 1) * pack)
        lo = hi = None
        for k in range(TOP_K):
            words = ybuf_ref[k, :, cols]
            lo_k = w[:, k:k + 1] * lax.bitcast_convert_type(words << 16, F32)
            hi_k = w[:, k:k + 1] * lax.bitcast_convert_type(words & U32(HIGH_HALF), F32)
            lo = lo_k if lo is None else lo + lo_k
            hi = hi_k if hi is None else hi + hi_k
        parts += [lo, hi]
    h2 = h1_ref[...] + jnp.concatenate(parts, axis=1)
    xn = _rms(h2, pg_ref[...]).astype(BF16)
    gate = jax.nn.sigmoid(_dot(xn, wpg_ref[...]))
    h3 = h2 + pw * gate
    o_ref[...] = _rms(h3, fg_ref[...]) if final else h3


def _combine(dest, wts, h1, p2, wple, wpg, pgain, fgain, yg, final):
    T, D = h1.shape
    tm = min(COMBINE_TM, T)
    row = lambda a: pl.BlockSpec((tm, a.shape[1]), lambda i: (i, 0))
    return pl.pallas_call(
        functools.partial(_combine_kernel, tm=tm, pack=min(EXPERT_TN_DOWN, D) // 2, final=final),
        grid=(T // tm,),
        in_specs=[pl.BlockSpec((8, tm), lambda i: (0, i), memory_space=pltpu.SMEM),
                  row(wts), row(h1), row(p2),
                  _resident(wple), _resident(wpg), _resident(pgain), _resident(fgain),
                  pl.BlockSpec(memory_space=pl.ANY)],
        out_specs=pl.BlockSpec((tm, D), lambda i: (i, 0)),
        out_shape=SDS((T, D), F32),
        scratch_shapes=[pltpu.VMEM((TOP_K, tm, D // 2), U32), pltpu.SemaphoreType.DMA],
        compiler_params=_params(1),
        name="combine",
    )(dest, wts, h1, p2, wple, wpg, pgain, fgain, yg)


def kernel(x, p, attn_norm, w_in, q_lat_norm, kv_lat_norm, w_uq, w_ukv, w_o_mla, w_o_moba, w_out,
           rel_bias, moe_norm, w_router, b_router, w_gate, b_gate, w_up, b_up, w_down, b_down,
           ple_norm, w_ple_gate, w_ple, final_norm):
    B, S, D = x.shape
    T = B * S
    n_layers = w_in.shape[0]
    E = w_router.shape[-1]
    H = MLA_HEADS
    mw = MOBA_HEADS * MOBA_HEAD_DIM
    assert S % MOBA_BLOCK == 0 and E <= LANES
    o_kr = MLA_Q_LORA + MLA_KV_LORA
    o_q = o_kr + MLA_ROPE
    o_g = o_q + 3 * mw

    inv = 1.0 / (ROPE_THETA ** (jnp.arange(0, MLA_ROPE, 2, dtype=F32) / MLA_ROPE))
    ang = jnp.arange(S, dtype=F32)[:, None] * inv[None, :]
    cos, sin = jnp.cos(ang), jnp.sin(ang)
    zpad = jnp.zeros((S, LANES - MLA_ROPE), F32)
    cosw = jnp.concatenate([cos, cos, zpad], axis=1)
    sinw = jnp.concatenate([-sin, sin, zpad], axis=1)
    r = jnp.arange(MOBA_BLOCK)
    d0 = r[None, :] - r[:, None]
    bidx = jnp.stack([_t5_bucket(d0), _t5_bucket(d0 + MOBA_BLOCK)]).astype(I32)
    et = (jnp.arange(S)[:, None] // MOBA_BLOCK == jnp.arange(LANES)[None, :]).astype(BF16)
    bias = _moba_bias(rel_bias, bidx)

    h = x.reshape(T, D)
    for li in range(n_layers):
        w = w_in[li]
        w_main = jnp.concatenate([w[:, :o_kr], w[:, o_q:]], axis=1).astype(BF16)
        w_kr = jnp.pad(w[:, o_kr:o_q], ((0, 0), (0, LANES - MLA_ROPE))).astype(BF16)
        wq = w_uq[li].reshape(MLA_Q_LORA, H, MLA_NOPE + MLA_ROPE)
        wq = jnp.pad(wq, ((0, 0), (0, 0), (0, MLA_SLOT - MLA_NOPE - MLA_ROPE)))
        wq = wq.reshape(MLA_Q_LORA, H * MLA_SLOT).astype(BF16)
        wkv = w_ukv[li].reshape(MLA_KV_LORA, H, MLA_NOPE + MLA_V)
        wk = wkv[:, :, :MLA_NOPE].reshape(MLA_KV_LORA, H * MLA_NOPE).astype(BF16)
        wv = wkv[:, :, MLA_NOPE:].reshape(MLA_KV_LORA, H * MLA_V).astype(BF16)
        wr = jnp.pad(w_router[li], ((0, 0), (0, LANES - E))).astype(BF16)
        br = jnp.pad(b_router[li], (0, LANES - E), constant_values=NEG_INF)[None, :]

        lat, qkv, gates, kr = _in_proj(h, attn_norm[li][None, :], w_main, w_kr, o_kr, 3 * mw)
        q_a, k_a, v_a = _mla_proj(lat, kr, cosw, sinw, q_lat_norm[li][None, :], kv_lat_norm[li][None, :],
                                  wq, wk, wv, S)
        y_a = _mla_attn(q_a, k_a, v_a, B, S)
        y_b = _moba_attn(qkv, et, bias, rel_bias, B, S)
        h1, xn, logits = _out_proj(y_a, y_b, gates, h, w_o_mla[li].astype(BF16), w_o_moba[li].astype(BF16),
                                   w_out[li].astype(BF16), moe_norm[li][None, :], wr, br)

        e_k, w_k, r_k, cnt = _route(logits, E)
        counts = cnt[:E, 0].astype(I32)
        tiles = (counts + EXPERT_ROWS - 1) // EXPERT_ROWS
        tile_end = jnp.cumsum(tiles)
        tile_start = tile_end - tiles
        n_tiles = (T * TOP_K) // EXPERT_ROWS + E
        n_used = tile_end[-1]
        tile_ids = jnp.arange(n_tiles)
        capped = jnp.minimum(tile_ids, n_used - 1)
        tile_expert = jnp.minimum(jnp.sum(tile_end[None, :] <= capped[:, None], axis=1), E - 1).astype(I32)
        base = jnp.pad(tile_start * EXPERT_ROWS, (0, LANES - E)).astype(I32)
        dest = _dest(e_k, r_k, jnp.broadcast_to(base[:, None], (LANES, LANES)))
        mine = tile_expert[:, None] == jnp.arange(E)[None, :]
        in_tile = jnp.sum(jnp.where(mine, counts[None, :] - (tile_ids[:, None] - tile_start[None, :]) * EXPERT_ROWS, 0),
                          axis=1)
        tile_valid = jnp.where(tile_ids < n_used, jnp.clip(in_tile, 0, EXPERT_ROWS), 0).astype(I32)
        chunk_lo = jnp.arange(n_tiles * EXPERT_ROWS // ZERO_ROWS) * ZERO_ROWS
        pad_lo = tile_start * EXPERT_ROWS + counts
        pad_hi = tile_end * EXPERT_ROWS
        in_pad = (chunk_lo[:, None] < pad_hi[None, :]) & (chunk_lo[:, None] + ZERO_ROWS > pad_lo[None, :])
        zflag = (jnp.any(in_pad, axis=1) | (chunk_lo >= n_used * EXPERT_ROWS)).astype(I32)
        nu = n_used.astype(I32)[None]

        xg = _dispatch(zflag, dest, xn)
        hid = _ffn_up(tile_expert, tile_valid, nu, xg, w_gate[li], w_up[li],
                      b_gate[li][:, None, :], b_up[li][:, None, :])
        yg = _ffn_down(tile_expert, tile_valid, nu, hid, w_down[li], b_down[li][:, None, :])
        h = _combine(dest, w_k[:TOP_K].T, h1, p[li].reshape(T, -1), w_ple[li].astype(BF16),
                     w_ple_gate[li].astype(BF16), ple_norm[li][None, :], final_norm[None, :], yg,
                     final=li == n_layers - 1)
    return h.reshape(B, S, D)
```

```python
import functools
import math

import jax
import jax.numpy as jnp
from jax import lax
from jax.experimental import pallas as pl
from jax.experimental.pallas import tpu as pltpu

F32 = jnp.float32
BF16 = jnp.bfloat16
I32 = jnp.int32
U32 = jnp.uint32
HIGH_HALF = 0xFFFF0000
SDS = jax.ShapeDtypeStruct

EPS = 1e-6
NEG_INF = -1e30
LOG2E = math.log2(math.e)
MLA_HEADS = 8
MLA_NOPE = 128
MLA_ROPE = 64
MLA_V = 128
MLA_Q_LORA = 512
MLA_KV_LORA = 512
ROPE_THETA = 10000.0
MOBA_HEADS = 8
MOBA_HEAD_DIM = 128
MOBA_BLOCK = 256
MOBA_TOPK = 3
REL_BUCKETS = 32
REL_MAX_DIST = 128
TOP_K = 4
SWIGLU_LIMIT = 7.0
SWIGLU_ALPHA = 1.702

LANES = 128
MLA_SLOT = 2 * LANES
VMEM_LIMIT = 56 * 2**20

IN_TM, IN_TN = 1024, 512
MLAP_TM = 512
MLA_TQ = 512
MLA_TK = 512
MLA_HEADS_PER_STEP = 4
MOBA_HEADS_PER_STEP = 4
OUT_TM = 256
ROUTE_TM = 512
EXPERT_ROWS = 1024
EXPERT_SUB = 256
EXPERT_TN_UP = 512
EXPERT_TN_DOWN = 1024
ZERO_ROWS = 512
DISPATCH_TM = 512
DEST_TB = 2048
COMBINE_TM = 256

_NT = (((1,), (1,)), ((), ()))


def _params(n_axes):
    return pltpu.CompilerParams(dimension_semantics=("arbitrary",) * n_axes,
                                vmem_limit_bytes=VMEM_LIMIT)


def _rms(x, g):
    return x * lax.rsqrt(jnp.mean(x * x, axis=-1, keepdims=True) + EPS) * g


def _dot(a, b):
    return jnp.dot(a, b, preferred_element_type=F32)


def _in_proj_kernel(x_ref, g_ref, w_ref, wkr_ref, lat_ref, qkv_ref, gates_ref, kr_ref, xn_ref,
                    *, n_lat, n_qkv):
    j = pl.program_id(1)

    @pl.when(j == 0)
    def _():
        xn = _rms(x_ref[...], g_ref[...]).astype(BF16)
        xn_ref[...] = xn
        kr_ref[...] = _dot(xn, wkr_ref[...])

    @pl.when(j < n_lat)
    def _():
        lat_ref[...] = _dot(xn_ref[...], w_ref[...])

    @pl.when((j >= n_lat) & (j < n_lat + n_qkv))
    def _():
        qkv_ref[...] = _dot(xn_ref[...], w_ref[...]).astype(BF16)

    @pl.when(j >= n_lat + n_qkv)
    def _():
        gates_ref[...] = _dot(xn_ref[...], w_ref[...])


def _in_proj(x2, gain, w_main, w_kr, n_lat_cols, n_qkv_cols):
    T, D = x2.shape
    n_g_cols = w_main.shape[1] - n_lat_cols - n_qkv_cols
    tm = min(IN_TM, T)
    tn = math.gcd(math.gcd(IN_TN, n_lat_cols), math.gcd(n_qkv_cols, n_g_cols))
    n_lat, n_qkv, n_g = n_lat_cols // tn, n_qkv_cols // tn, n_g_cols // tn
    return pl.pallas_call(
        functools.partial(_in_proj_kernel, n_lat=n_lat, n_qkv=n_qkv),
        grid=(T // tm, n_lat + n_qkv + n_g),
        in_specs=[pl.BlockSpec((tm, D), lambda i, j: (i, 0)),
                  pl.BlockSpec((1, D), lambda i, j: (0, 0)),
                  pl.BlockSpec((D, tn), lambda i, j: (0, j)),
                  pl.BlockSpec((D, LANES), lambda i, j: (0, 0))],
        out_specs=[pl.BlockSpec((tm, tn), lambda i, j: (i, jnp.minimum(j, n_lat - 1))),
                   pl.BlockSpec((tm, tn), lambda i, j: (i, jnp.clip(j - n_lat, 0, n_qkv - 1))),
                   pl.BlockSpec((tm, tn), lambda i, j: (i, jnp.maximum(j - n_lat - n_qkv, 0))),
                   pl.BlockSpec((tm, LANES), lambda i, j: (i, 0))],
        out_shape=[SDS((T, n_lat_cols), F32), SDS((T, n_qkv_cols), BF16),
                   SDS((T, n_g_cols), F32), SDS((T, LANES), F32)],
        scratch_shapes=[pltpu.VMEM((tm, D), BF16)],
        compiler_params=_params(2),
        name="in_proj",
    )(x2, gain, w_main, w_kr)


def _mla_proj_kernel(lat_ref, kr_ref, cos_ref, sin_ref, qn_ref, kvn_ref, wq_ref, wk_ref, wv_ref,
                     q_ref, k_ref, v_ref):
    lat = lat_ref[...]
    qn = _rms(lat[:, :MLA_Q_LORA], qn_ref[...]).astype(BF16)
    kvn = _rms(lat[:, MLA_Q_LORA:], kvn_ref[...]).astype(BF16)
    q = _dot(qn, wq_ref[...])
    kn = _dot(kvn, wk_ref[...])
    v_ref[...] = _dot(kvn, wv_ref[...]).astype(BF16)
    c = cos_ref[...]
    s = sin_ref[...]
    half = MLA_ROPE // 2
    lane = lax.broadcasted_iota(I32, c.shape, 1)

    def rope(xr):
        swapped = jnp.where(lane < half, pltpu.roll(xr, LANES - half, 1), pltpu.roll(xr, half, 1))
        return xr * c + swapped * s

    kr = rope(kr_ref[...]).astype(BF16)
    for h in range(MLA_HEADS):
        lo = h * MLA_SLOT
        q_ref[:, lo:lo + LANES] = q[:, lo:lo + LANES].astype(BF16)
        q_ref[:, lo + LANES:lo + MLA_SLOT] = rope(q[:, lo + LANES:lo + MLA_SLOT]).astype(BF16)
        k_ref[:, lo:lo + LANES] = kn[:, h * MLA_NOPE:(h + 1) * MLA_NOPE].astype(BF16)
        k_ref[:, lo + LANES:lo + MLA_SLOT] = kr


def _mla_proj(lat, kr, cosw, sinw, qnorm, kvnorm, wq, wk, wv, S):
    T = lat.shape[0]
    tm = min(MLAP_TM, S)
    ns = S // tm
    H = MLA_HEADS
    full = lambda a: pl.BlockSpec(a.shape, lambda i: (0,) * a.ndim)
    return pl.pallas_call(
        _mla_proj_kernel,
        grid=(T // tm,),
        in_specs=[pl.BlockSpec((tm, lat.shape[1]), lambda i: (i, 0)),
                  pl.BlockSpec((tm, LANES), lambda i: (i, 0)),
                  pl.BlockSpec((tm, LANES), lambda i: (i % ns, 0)),
                  pl.BlockSpec((tm, LANES), lambda i: (i % ns, 0)),
                  full(qnorm), full(kvnorm), full(wq), full(wk), full(wv)],
        out_specs=[pl.BlockSpec((tm, H * MLA_SLOT), lambda i: (i, 0)),
                   pl.BlockSpec((tm, H * MLA_SLOT), lambda i: (i, 0)),
                   pl.BlockSpec((tm, H * MLA_V), lambda i: (i, 0))],
        out_shape=[SDS((T, H * MLA_SLOT), BF16), SDS((T, H * MLA_SLOT), BF16),
                   SDS((T, H * MLA_V), BF16)],
        compiler_params=_params(1),
        name="mla_proj",
    )(lat, kr, cosw, sinw, qnorm, kvnorm, wq, wk, wv)


def _transpose_chunks(src_ref, dst_ref):
    n, _, tk = dst_ref.shape
    for c in range(n):
        dst_ref[c] = src_ref[c * tk:(c + 1) * tk, :].astype(F32).T.astype(dst_ref.dtype)


def _online_softmax_t(carry, s, vt):
    return _online_softmax_heads((carry,), (lambda: s,), (vt,))[0]


def _online_softmax_heads(carries, score_fns, vts):
    n = len(carries)
    scores, stats, out = {}, {}, [None] * n
    for step in range(n + 2):
        if step < n:
            scores[step] = score_fns[step]()
        g = step - 1
        if 0 <= g < n:
            m, l, _ = carries[g]
            s = scores.pop(g)
            m_new = jnp.maximum(m, jnp.max(s, axis=0, keepdims=True))
            alpha = jnp.exp2(m - m_new)
            p = jnp.exp2(s - m_new)
            stats[g] = (m_new, alpha * l + jnp.sum(p, axis=0, keepdims=True), alpha, p.astype(BF16))
        g = step - 2
        if 0 <= g < n:
            m_new, l, alpha, p = stats.pop(g)
            chunks = vts[g] if isinstance(vts[g], (tuple, list)) else (vts[g],)
            rows = p.shape[0] // len(chunks)
            pv = _dot(chunks[0], p[:rows])
            for c in range(1, len(chunks)):
                pv = pv + _dot(chunks[c], p[c * rows:(c + 1) * rows])
            out[g] = (m_new, l, alpha * carries[g][2] + pv)
    return tuple(out)


def _mla_attn_kernel(q_ref, k_ref, v_ref, o_ref, vt_ref, *, tq, tk, scale, heads):
    qi = pl.program_id(2)
    slot, dv = MLA_SLOT, MLA_V

    @pl.when(qi == 0)
    def _():
        for g in range(heads):
            _transpose_chunks(v_ref.at[:, g * dv:(g + 1) * dv], vt_ref.at[g])

    qs = [q_ref[:, g * slot:(g + 1) * slot] for g in range(heads)]
    per_q = tq // tk

    def chunk(g, c):
        k = k_ref[pl.ds(pl.multiple_of(c * tk, tk), tk), g * slot:(g + 1) * slot]
        return lax.dot_general(k, qs[g], _NT, preferred_element_type=F32) * (scale * LOG2E), vt_ref[g, c]

    carry = [None] * heads
    for c in range(per_q):
        for g in range(heads):
            s, vt = chunk(g, qi * per_q + c)
            key = lax.broadcasted_iota(I32, s.shape, 0) + c * tk
            qry = lax.broadcasted_iota(I32, s.shape, 1)
            s = jnp.where(key <= qry, s, NEG_INF)
            if carry[g] is None:
                m = jnp.max(s, axis=0, keepdims=True)
                p = jnp.exp2(s - m)
                carry[g] = (m, jnp.sum(p, axis=0, keepdims=True), _dot(vt, p.astype(BF16)))
            else:
                carry[g] = _online_softmax_t(carry[g], s, vt)

    def body(c, carry):
        fns = [lambda g=g: chunk(g, c)[0] for g in range(heads)]
        return _online_softmax_heads(carry, fns, [vt_ref[g, c] for g in range(heads)])

    carry = lax.fori_loop(0, qi * per_q, body, tuple(carry))
    for g in range(heads):
        m, l, acc = carry[g]
        o_ref[:, g * dv:(g + 1) * dv] = (acc / l).T.astype(BF16)


def _mla_attn(q, k, v, B, S):
    H = MLA_HEADS
    G = MLA_HEADS_PER_STEP
    T = B * S
    tq = min(MLA_TQ, S)
    tk = min(MLA_TK, tq)
    nq = S // tq
    scale = (MLA_NOPE + MLA_ROPE) ** -0.5
    return pl.pallas_call(
        functools.partial(_mla_attn_kernel, tq=tq, tk=tk, scale=scale, heads=G),
        grid=(B, H // G, nq),
        in_specs=[pl.BlockSpec((tq, G * MLA_SLOT), lambda b, h, i: (b * nq + i, h)),
                  pl.BlockSpec((S, G * MLA_SLOT), lambda b, h, i: (b, h)),
                  pl.BlockSpec((S, G * MLA_V), lambda b, h, i: (b, h))],
        out_specs=pl.BlockSpec((tq, G * MLA_V), lambda b, h, i: (b * nq + i, h)),
        out_shape=SDS((T, H * MLA_V), BF16),
        scratch_shapes=[pltpu.VMEM((G, S // tk, MLA_V, tk), BF16)],
        compiler_params=_params(3),
        name="mla_attn",
    )(q, k, v)


def _t5_bucket(dist):
    n = jnp.maximum(dist, 0)
    max_exact = REL_BUCKETS // 2
    large = max_exact + (jnp.log(jnp.maximum(n, 1).astype(F32) / max_exact)
                         / math.log(REL_MAX_DIST / max_exact)
                         * (REL_BUCKETS - max_exact)).astype(I32)
    large = jnp.minimum(large, REL_BUCKETS - 1)
    return jnp.where(n < max_exact, n, large)


def _moba_bias_kernel(rb_ref, bidx_ref, o_ref):
    h = pl.program_id(0)
    for t in range(2):
        bi = bidx_ref[t]
        val = jnp.zeros(bi.shape, F32)
        for b in range(REL_BUCKETS):
            val = jnp.where(bi == b, rb_ref[b, h], val)
        o_ref[t] = val * LOG2E


def _moba_bias(rel_bias, bidx):
    H = rel_bias.shape[1]
    blk = MOBA_BLOCK
    return pl.pallas_call(
        _moba_bias_kernel,
        grid=(H,),
        in_specs=[pl.BlockSpec(memory_space=pltpu.SMEM),
                  pl.BlockSpec((2, blk, blk), lambda h: (0, 0, 0))],
        out_specs=pl.BlockSpec((None, 2, blk, blk), lambda h: (h, 0, 0, 0)),
        out_shape=SDS((H, 2, blk, blk), F32),
        compiler_params=_params(1),
        name="moba_bias",
    )(rel_bias, bidx)


def _moba_attn_kernel(rb_ref, q_ref, k_ref, v_ref, et_ref, bias_ref, o_ref, ka_ref, km_ref, qa_ref, vt_ref,
                      *, nb, n_sel, scale, heads):
    hg = pl.program_id(1)
    i = pl.program_id(2)
    blk = MOBA_BLOCK
    d = MOBA_HEAD_DIM
    nbp = -(-nb // 8) * 8

    @pl.when(i == 0)
    def _():
        for g in range(heads):
            cols = slice(g * d, (g + 1) * d)
            ka_ref[g, :, :d] = k_ref[:, cols]
            ka_ref[g, :, d:] = et_ref[...]
            _transpose_chunks(v_ref.at[:, cols], vt_ref.at[g])
            km_ref[g] = jnp.zeros(km_ref.shape[1:], F32)
            for n in range(nb):
                kb = k_ref[n * blk:(n + 1) * blk, cols].astype(F32)
                km_ref[g, n:n + 1, :] = jnp.sum(kb, axis=0, keepdims=True) * (1.0 / blk)

    qas = []
    for g in range(heads):
        q = q_ref[:, g * d:(g + 1) * d]
        gate = lax.dot_general(km_ref[g, :nbp, :].astype(BF16), q, _NT, preferred_element_type=F32)
        sub = lax.broadcasted_iota(I32, gate.shape, 0)
        gt = jnp.where(sub < i, gate, NEG_INF)
        keep = jnp.full(gate.shape, NEG_INF, F32)
        for _ in range(n_sel):
            mx = jnp.max(gt, axis=0, keepdims=True)
            first = jnp.min(jnp.where(gt == mx, sub, LANES), axis=0, keepdims=True)
            pick = sub == first
            keep = jnp.where(pick & (sub < i), 0.0, keep)
            gt = jnp.where(pick, -3.0e38, gt)
        keep = jnp.where(sub == i, 0.0, keep)
        keep = jnp.concatenate([keep, jnp.full((LANES - nbp, blk), NEG_INF, F32)], axis=0)
        qa_ref[g, :, :d] = q
        qa_ref[g, :, d:] = keep.T.astype(BF16)
        qas.append(qa_ref[g])

    def block(g, n):
        kk = ka_ref[g, pl.ds(pl.multiple_of(n * blk, blk), blk), :]
        return lax.dot_general(kk, qas[g], _NT, preferred_element_type=F32) * (scale * LOG2E), vt_ref[g, n]

    carry = []
    for g in range(heads):
        s, vt = block(g, i)
        s = s + bias_ref[g, 0]
        key = lax.broadcasted_iota(I32, s.shape, 0)
        qry = lax.broadcasted_iota(I32, s.shape, 1)
        s = jnp.where(key <= qry, s, NEG_INF)
        m = jnp.max(s, axis=0, keepdims=True)
        p = jnp.exp2(s - m)
        carry.append((m, jnp.sum(p, axis=0, keepdims=True), _dot(vt, p.astype(BF16))))

    def adjacent(n, carry):
        fns = [lambda g=g: block(g, n)[0] + bias_ref[g, 1] for g in range(heads)]
        return _online_softmax_heads(carry, fns, [vt_ref[g, n] for g in range(heads)])

    far_bias = [rb_ref[REL_BUCKETS - 1, hg * heads + g] * LOG2E for g in range(heads)]

    def far(n, carry):
        fns = [lambda g=g: block(g, n)[0] + far_bias[g] for g in range(heads)]
        return _online_softmax_heads(carry, fns, [vt_ref[g, n] for g in range(heads)])

    def far_pair(j, carry):
        def scores(g):
            kk = ka_ref[g, pl.ds(pl.multiple_of(j * (2 * blk), 2 * blk), 2 * blk), :]
            s = lax.dot_general(kk, qas[g], _NT, preferred_element_type=F32)
            return s * (scale * LOG2E) + far_bias[g]
        fns = [lambda g=g: scores(g) for g in range(heads)]
        return _online_softmax_heads(carry, fns, [(vt_ref[g, 2 * j], vt_ref[g, 2 * j + 1]) for g in range(heads)])

    n_far = jnp.maximum(i - 1, 0)
    n_pair = lax.shift_right_logical(n_far, 1)
    carry = lax.fori_loop(n_far, i, adjacent, tuple(carry))
    carry = lax.fori_loop(0, n_pair, far_pair, carry)
    carry = lax.fori_loop(2 * n_pair, n_far, far, carry)
    for g in range(heads):
        m, l, acc = carry[g]
        o_ref[:, g * d:(g + 1) * d] = (acc / l).T.astype(BF16)


def _moba_attn(qkv, et, bias, rel_bias, B, S):
    H, d, blk = MOBA_HEADS, MOBA_HEAD_DIM, MOBA_BLOCK
    T = B * S
    nb = S // blk
    n_sel = min(MOBA_TOPK, nb)
    G = MOBA_HEADS_PER_STEP
    ng = H // G
    return pl.pallas_call(
        functools.partial(_moba_attn_kernel, nb=nb, n_sel=n_sel, scale=d ** -0.5, heads=G),
        grid=(B, ng, nb),
        in_specs=[pl.BlockSpec(memory_space=pltpu.SMEM),
                  pl.BlockSpec((blk, G * d), lambda b, h, i: (b * nb + i, h)),
                  pl.BlockSpec((S, G * d), lambda b, h, i: (b, ng + h)),
                  pl.BlockSpec((S, G * d), lambda b, h, i: (b, 2 * ng + h)),
                  pl.BlockSpec((S, LANES), lambda b, h, i: (0, 0)),
                  pl.BlockSpec((G, 2, blk, blk), lambda b, h, i: (h, 0, 0, 0))],
        out_specs=pl.BlockSpec((blk, G * d), lambda b, h, i: (b * nb + i, h)),
        out_shape=SDS((T, H * d), BF16),
        scratch_shapes=[pltpu.VMEM((G, S, d + LANES), BF16),
                        pltpu.VMEM((G, LANES, d), F32),
                        pltpu.VMEM((G, blk, d + LANES), BF16),
                        pltpu.VMEM((G, nb, d, blk), BF16)],
        compiler_params=_params(3),
        name="moba_attn",
    )(rel_bias, qkv, qkv, qkv, et, bias)


def _out_proj_kernel(ya_ref, yb_ref, g_ref, x_ref, woa_ref, wob_ref, wout_ref, mg_ref, wr_ref, br_ref,
                     h1_ref, xn_ref, lg_ref, *, D):
    a = _dot(ya_ref[...], woa_ref[...])
    b = _dot(yb_ref[...], wob_ref[...])
    g = g_ref[...]
    merged = jax.nn.sigmoid(g[:, :D]) * a + jax.nn.sigmoid(g[:, D:]) * b
    h1 = x_ref[...] + _dot(merged.astype(BF16), wout_ref[...])
    h1_ref[...] = h1
    xn = _rms(h1, mg_ref[...]).astype(BF16)
    lg_ref[...] = _dot(xn, wr_ref[...]) + br_ref[...]
    packed = _pack_bf16_pairs(xn)
    tm, r = packed.shape[0], packed.shape[1] // LANES
    for s in range(r):
        xn_ref[pl.ds(s, tm, stride=r), :] = packed[:, s * LANES:(s + 1) * LANES]


def _pack_bf16_pairs(x):
    bits = lax.bitcast_convert_type(x.astype(F32), U32)
    half = x.shape[1] // 2
    return (bits[:, :half] >> 16) | (bits[:, half:] & U32(HIGH_HALF))


def _unpack_bf16_pairs(w):
    lo = lax.bitcast_convert_type(w << 16, F32).astype(BF16)
    hi = lax.bitcast_convert_type(w & U32(HIGH_HALF), F32).astype(BF16)
    return lo, hi


def _resident(a):
    return pl.BlockSpec(a.shape, lambda i: (0,) * a.ndim, pipeline_mode=pl.Buffered(1))


def _out_proj(ya, yb, gates, x2, woa, wob, wout, mgain, wr, br):
    T, D = x2.shape
    tm = min(OUT_TM, T)
    row = lambda a: pl.BlockSpec((tm, a.shape[1]), lambda i: (i, 0))
    return pl.pallas_call(
        functools.partial(_out_proj_kernel, D=D),
        grid=(T // tm,),
        in_specs=[row(ya), row(yb), row(gates), row(x2),
                  _resident(woa), _resident(wob), _resident(wout), _resident(mgain),
                  _resident(wr), _resident(br)],
        out_specs=[pl.BlockSpec((tm, D), lambda i: (i, 0)),
                   pl.BlockSpec((tm * (D // 2 // LANES), LANES), lambda i: (i, 0)),
                   pl.BlockSpec((tm, LANES), lambda i: (i, 0))],
        out_shape=[SDS((T, D), F32), SDS((T * (D // 2 // LANES), LANES), U32), SDS((T, LANES), F32)],
        compiler_params=_params(1),
        name="out_proj",
    )(ya, yb, gates, x2, woa, wob, wout, mgain, wr, br)


def _route_kernel(lg_ref, e_ref, w_ref, r_ref, cnt_ref, carry_ref, *, n_exp):
    i = pl.program_id(0)

    @pl.when(i == 0)
    def _():
        carry_ref[...] = jnp.zeros(carry_ref.shape, F32)

    lt = lg_ref[...].T
    tm = lt.shape[1]
    sub = lax.broadcasted_iota(I32, lt.shape, 0)
    neg = -jnp.inf
    cur = jnp.where(sub < n_exp, lt, neg)
    vals, idxs = [], []
    for _ in range(TOP_K):
        mx = jnp.max(cur, axis=0, keepdims=True)
        ix = jnp.min(jnp.where(cur == mx, sub, LANES), axis=0, keepdims=True)
        vals.append(mx)
        idxs.append(ix)
        cur = jnp.where(sub == ix, neg, cur)
    ex = [jnp.exp(v - vals[0]) for v in vals]
    den = ex[0]
    for e in ex[1:]:
        den = den + e
    onehot = jnp.zeros(lt.shape, F32)
    for ix in idxs:
        onehot = onehot + (sub == ix).astype(F32)
    r_i = lax.broadcasted_iota(I32, (tm, tm), 0)
    c_i = lax.broadcasted_iota(I32, (tm, tm), 1)
    tri = (r_i <= c_i).astype(BF16)
    incl = _dot(onehot.astype(BF16), tri)
    base = carry_ref[...]
    excl = incl - onehot + base[:, :1]
    e_ref[...] = jnp.zeros(e_ref.shape, I32)
    w_ref[...] = jnp.zeros(w_ref.shape, F32)
    r_ref[...] = jnp.zeros(r_ref.shape, I32)
    for k in range(TOP_K):
        e_ref[k:k + 1, :] = idxs[k]
        w_ref[k:k + 1, :] = ex[k] / den
        rk = jnp.sum(jnp.where(sub == idxs[k], excl, 0.0), axis=0, keepdims=True)
        r_ref[k:k + 1, :] = rk.astype(I32)
    total = base + jnp.sum(onehot, axis=1, keepdims=True)
    carry_ref[...] = total
    cnt_ref[...] = total


def _route(logits, n_exp):
    T = logits.shape[0]
    tm = min(ROUTE_TM, T)
    return pl.pallas_call(
        functools.partial(_route_kernel, n_exp=n_exp),
        grid=(T // tm,),
        in_specs=[pl.BlockSpec((tm, LANES), lambda i: (i, 0))],
        out_specs=[pl.BlockSpec((8, tm), lambda i: (0, i)),
                   pl.BlockSpec((8, tm), lambda i: (0, i)),
                   pl.BlockSpec((8, tm), lambda i: (0, i)),
                   pl.BlockSpec((LANES, LANES), lambda i: (0, 0))],
        out_shape=[SDS((8, T), I32), SDS((8, T), F32), SDS((8, T), I32), SDS((LANES, LANES), F32)],
        scratch_shapes=[pltpu.VMEM((LANES, LANES), F32)],
        compiler_params=_params(1),
        name="route",
    )(logits)


def _dest_kernel(e_ref, r_ref, base_ref, d_ref):
    base = base_ref[...][:, :1]
    sub = lax.broadcasted_iota(I32, (LANES, e_ref.shape[1]), 0)
    d_ref[...] = jnp.zeros(d_ref.shape, I32)
    for k in range(TOP_K):
        off = jnp.sum(jnp.where(sub == e_ref[k:k + 1, :], base, 0), axis=0, keepdims=True)
        d_ref[k:k + 1, :] = r_ref[k:k + 1, :] + off


def _dest(e_k, r_k, base):
    T = e_k.shape[1]
    tb = min(DEST_TB, T)
    blk = pl.BlockSpec((8, tb), lambda i: (0, i))
    return pl.pallas_call(
        _dest_kernel,
        grid=(T // tb,),
        in_specs=[blk, blk, pl.BlockSpec((LANES, LANES), lambda i: (0, 0))],
        out_specs=blk,
        out_shape=SDS((8, T), I32),
        compiler_params=_params(1),
        name="dest",
    )(e_k, r_k, base)


def _dispatch_kernel(zflag_ref, dest_ref, x_ref, xg_hbm, zbuf_ref, sem, *, tm, rows, n_tiles, r):
    i = pl.program_id(0)
    rows = rows * r

    @pl.when(i == 0)
    def _():
        zbuf_ref[...] = jnp.zeros(zbuf_ref.shape, zbuf_ref.dtype)

        def zcopy(j):
            start = pl.multiple_of(j * rows, rows)
            return pltpu.make_async_copy(zbuf_ref, xg_hbm.at[pl.ds(start, rows)], sem)

        def zstart(j, c):
            @pl.when(zflag_ref[j] != 0)
            def _():
                zcopy(j).start()
            return c

        def zwait(j, c):
            @pl.when(zflag_ref[j] != 0)
            def _():
                zcopy(j).wait()
            return c

        lax.fori_loop(0, n_tiles, zstart, 0)
        lax.fori_loop(0, n_tiles, zwait, 0)

    def copy(t, k):
        src = x_ref.at[pl.ds(pl.multiple_of(t * r, r), r)]
        dst = xg_hbm.at[pl.ds(pl.multiple_of(dest_ref[k, t] * r, r), r)]
        return pltpu.make_async_copy(src, dst, sem)

    def start(t, c):
        for k in range(TOP_K):
            copy(t, k).start(priority=k % 2)
        return c

    def wait(t, c):
        for k in range(TOP_K):
            copy(t, k).wait()
        return c

    lax.fori_loop(0, tm, start, 0, unroll=4)
    lax.fori_loop(0, tm, wait, 0, unroll=4)


def _dispatch(zflag, dest, xn):
    T = dest.shape[1]
    r = xn.shape[0] // T
    tm = min(DISPATCH_TM, T)
    n_chunks = zflag.shape[0]
    return pl.pallas_call(
        functools.partial(_dispatch_kernel, tm=tm, rows=ZERO_ROWS, n_tiles=n_chunks, r=r),
        grid_spec=pltpu.PrefetchScalarGridSpec(
            num_scalar_prefetch=1,
            grid=(T // tm,),
            in_specs=[pl.BlockSpec((8, tm), lambda i, zf: (0, i), memory_space=pltpu.SMEM),
                      pl.BlockSpec((tm * r, LANES), lambda i, zf: (i, 0))],
            out_specs=pl.BlockSpec(memory_space=pl.ANY),
            scratch_shapes=[pltpu.VMEM((ZERO_ROWS * r, LANES), xn.dtype), pltpu.SemaphoreType.DMA]),
        out_shape=SDS((n_chunks * ZERO_ROWS * r, LANES), xn.dtype),
        compiler_params=_params(1),
        name="dispatch",
    )(zflag, dest, xn)


def _row_blocks(valid, n_rows, compute, out_ref):
    full = valid > n_rows - EXPERT_SUB

    @pl.when(full)
    def _():
        compute(slice(0, n_rows))

    for sb in range(n_rows // EXPERT_SUB):
        rows = slice(sb * EXPERT_SUB, (sb + 1) * EXPERT_SUB)

        @pl.when(jnp.logical_not(full) & (sb * EXPERT_SUB < valid))
        def _():
            compute(rows)

        @pl.when(jnp.logical_not(full) & (sb * EXPERT_SUB >= valid))
        def _():
            out_ref[rows, :] = jnp.zeros((EXPERT_SUB, out_ref.shape[1]), out_ref.dtype)


def _ffn_up_kernel(te_ref, tv_ref, nu_ref, x_ref, wg_ref, wu_ref, bg_ref, bu_ref, h_ref, wgb_ref, wub_ref):
    i = pl.program_id(1)
    valid = tv_ref[i]
    new_expert = (i == 0) | (te_ref[i] != te_ref[jnp.maximum(i - 1, 0)])

    @pl.when((valid > 0) & new_expert)
    def _():
        wgb_ref[...] = wg_ref[...].astype(BF16)
        wub_ref[...] = wu_ref[...].astype(BF16)

    r = x_ref.shape[0] // h_ref.shape[0]
    half = r * LANES

    def compute(rows):
        n = rows.stop - rows.start
        parts = [_unpack_bf16_pairs(x_ref[pl.ds(rows.start * r + s, n, stride=r), :]) for s in range(r)]
        lo = jnp.concatenate([lo_s for lo_s, _ in parts], axis=1)
        hi = jnp.concatenate([hi_s for _, hi_s in parts], axis=1)
        g = _dot(lo, wgb_ref[:half, :]) + _dot(hi, wgb_ref[half:, :]) + bg_ref[...]
        u = _dot(lo, wub_ref[:half, :]) + _dot(hi, wub_ref[half:, :]) + bu_ref[...]
        g = jnp.minimum(g, SWIGLU_LIMIT)
        u = jnp.clip(u, -SWIGLU_LIMIT, SWIGLU_LIMIT)
        h_ref[rows, :] = (g * jax.nn.sigmoid(SWIGLU_ALPHA * g) * (u + 1.0)).astype(BF16)

    _row_blocks(valid, h_ref.shape[0], compute, h_ref)


def _ffn_up(tile_expert, tile_valid, n_used, xg, wg, wu, bg, bu):
    D, Dx = wg.shape[1], wg.shape[2]
    r = D // 2 // LANES
    P = xg.shape[0] // r
    tr = EXPERT_ROWS
    tn = min(EXPERT_TN_UP, Dx)
    n_tiles = P // tr
    xmap = lambda j, i, te, tv, nu: (jnp.minimum(i, nu[0] - 1), 0)
    wmap = lambda j, i, te, tv, nu: (te[i], 0, j)
    return pl.pallas_call(
        _ffn_up_kernel,
        grid_spec=pltpu.PrefetchScalarGridSpec(
            num_scalar_prefetch=3,
            grid=(Dx // tn, n_tiles),
            in_specs=[pl.BlockSpec((tr * r, LANES), xmap),
                      pl.BlockSpec((None, D, tn), wmap),
                      pl.BlockSpec((None, D, tn), wmap),
                      pl.BlockSpec((None, 1, tn), wmap),
                      pl.BlockSpec((None, 1, tn), wmap)],
            out_specs=pl.BlockSpec((tr, tn), lambda j, i, te, tv, nu: (i, j)),
            scratch_shapes=[pltpu.VMEM((D, tn), BF16), pltpu.VMEM((D, tn), BF16)]),
        out_shape=SDS((P, Dx), BF16),
        compiler_params=_params(2),
        name="ffn_up",
    )(tile_expert, tile_valid, n_used, xg, wg, wu, bg, bu)


def _ffn_down_kernel(te_ref, tv_ref, nu_ref, h_ref, wd_ref, bd_ref, y_ref, wdb_ref):
    i = pl.program_id(1)
    valid = tv_ref[i]
    new_expert = (i == 0) | (te_ref[i] != te_ref[jnp.maximum(i - 1, 0)])

    @pl.when((valid > 0) & new_expert)
    def _():
        wdb_ref[...] = wd_ref[...].astype(BF16)

    def compute(rows):
        y = _dot(h_ref[rows, :], wdb_ref[...]) + bd_ref[...]
        y_ref[rows, :] = _pack_bf16_pairs(y.astype(BF16))

    _row_blocks(valid, h_ref.shape[0], compute, y_ref)


def _ffn_down(tile_expert, tile_valid, n_used, hid, wd, bd):
    P, Dx = hid.shape
    D = wd.shape[2]
    tr = EXPERT_ROWS
    tn = min(EXPERT_TN_DOWN, D)
    n_tiles = P // tr
    hmap = lambda j, i, te, tv, nu: (jnp.minimum(i, nu[0] - 1), 0)
    wmap = lambda j, i, te, tv, nu: (te[i], 0, j)
    return pl.pallas_call(
        _ffn_down_kernel,
        grid_spec=pltpu.PrefetchScalarGridSpec(
            num_scalar_prefetch=3,
            grid=(D // tn, n_tiles),
            in_specs=[pl.BlockSpec((tr, Dx), hmap),
                      pl.BlockSpec((None, Dx, tn), wmap),
                      pl.BlockSpec((None, 1, tn), wmap)],
            out_specs=pl.BlockSpec((tr, tn // 2), lambda j, i, te, tv, nu: (i, j)),
            scratch_shapes=[pltpu.VMEM((Dx, tn), BF16)]),
        out_shape=SDS((P, D // 2), U32),
        compiler_params=_params(2),
        name="ffn_down",
    )(tile_expert, tile_valid, n_used, hid, wd, bd)


def _combine_kernel(dest_ref, dnext_ref, w_ref, h1_ref, p_ref, wple_ref, wpg_ref, pg_ref, fg_ref, y_hbm,
                    o_ref, ybuf_ref, sems, *, tm, pack, final):
    i = pl.program_id(0)

    def copy(dref, col, k, slot, t):
        return pltpu.make_async_copy(y_hbm.at[pl.ds(dref[k, col], 1)],
                                     ybuf_ref.at[slot, k, pl.ds(t, 1)], sems.at[slot])

    def issue(dref, base, slot):
        for t in range(tm):
            for k in range(TOP_K):
                copy(dref, base + t, k, slot, t).start(priority=k % 2)

    def wait_all(slot):
        def wait(t, c):
            for k in range(TOP_K):
                copy(dest_ref, 0, k, slot, t).wait()
            return c
        lax.fori_loop(0, tm, wait, 0, unroll=4)

    def compute(rows, slot):
        w = w_ref[rows, :]
        pw = _dot(p_ref[rows, :].astype(BF16), wple_ref[...])
        parts = []
        for c in range(ybuf_ref.shape[3] // pack):
            cols = slice(c * pack, (c + 1) * pack)
            lo = hi = None
            for k in range(TOP_K):
                words = ybuf_ref[slot, k, :, cols]
                lo_k = w[:, k:k + 1] * lax.bitcast_convert_type(words << 16, F32)
                hi_k = w[:, k:k + 1] * lax.bitcast_convert_type(words & U32(HIGH_HALF), F32)
                lo = lo_k if lo is None else lo + lo_k
                hi = hi_k if hi is None else hi + hi_k
            parts += [lo, hi]
        h2 = h1_ref[rows, :] + jnp.concatenate(parts, axis=1)
        xn = _rms(h2, pg_ref[...]).astype(BF16)
        gate = jax.nn.sigmoid(_dot(xn, wpg_ref[...]))
        h3 = h2 + pw * gate
        o_ref[rows, :] = _rms(h3, fg_ref[...]) if final else h3

    @pl.when(i == 0)
    def _():
        issue(dest_ref, 0, 0)

    wait_all(0)
    issue(dest_ref, tm, 1)
    compute(slice(0, tm), 0)
    wait_all(1)
    issue(dnext_ref, 0, 0)
    compute(slice(tm, 2 * tm), 1)

    @pl.when(i == pl.num_programs(0) - 1)
    def _():
        wait_all(0)


def _combine(dest, wts, h1, p2, wple, wpg, pgain, fgain, yg, final):
    T, D = h1.shape
    tm = min(COMBINE_TM, T // 2)
    n = T // (2 * tm)
    row = lambda a: pl.BlockSpec((2 * tm, a.shape[1]), lambda i: (i, 0))
    return pl.pallas_call(
        functools.partial(_combine_kernel, tm=tm, pack=min(EXPERT_TN_DOWN, D) // 2, final=final),
        grid=(n,),
        in_specs=[pl.BlockSpec((8, 2 * tm), lambda i: (0, i), memory_space=pltpu.SMEM),
                  pl.BlockSpec((8, 2 * tm), lambda i: (0, jnp.minimum(i + 1, n - 1)), memory_space=pltpu.SMEM),
                  row(wts), row(h1), row(p2),
                  _resident(wple), _resident(wpg), _resident(pgain), _resident(fgain),
                  pl.BlockSpec(memory_space=pl.ANY)],
        out_specs=pl.BlockSpec((2 * tm, D), lambda i: (i, 0)),
        out_shape=SDS((T, D), F32),
        scratch_shapes=[pltpu.VMEM((2, TOP_K, tm, D // 2), U32), pltpu.SemaphoreType.DMA((2,))],
        compiler_params=_params(1),
        name="combine",
    )(dest, dest, wts, h1, p2, wple, wpg, pgain, fgain, yg)


def kernel(x, p, attn_norm, w_in, q_lat_norm, kv_lat_norm, w_uq, w_ukv, w_o_mla, w_o_moba, w_out,
           rel_bias, moe_norm, w_router, b_router, w_gate, b_gate, w_up, b_up, w_down, b_down,
           ple_norm, w_ple_gate, w_ple, final_norm):
    B, S, D = x.shape
    T = B * S
    n_layers = w_in.shape[0]
    E = w_router.shape[-1]
    H = MLA_HEADS
    mw = MOBA_HEADS * MOBA_HEAD_DIM
    assert S % MOBA_BLOCK == 0 and E <= LANES
    o_kr = MLA_Q_LORA + MLA_KV_LORA
    o_q = o_kr + MLA_ROPE
    o_g = o_q + 3 * mw

    inv = 1.0 / (ROPE_THETA ** (jnp.arange(0, MLA_ROPE, 2, dtype=F32) / MLA_ROPE))
    ang = jnp.arange(S, dtype=F32)[:, None] * inv[None, :]
    cos, sin = jnp.cos(ang), jnp.sin(ang)
    zpad = jnp.zeros((S, LANES - MLA_ROPE), F32)
    cosw = jnp.concatenate([cos, cos, zpad], axis=1)
    sinw = jnp.concatenate([-sin, sin, zpad], axis=1)
    r = jnp.arange(MOBA_BLOCK)
    d0 = r[None, :] - r[:, None]
    bidx = jnp.stack([_t5_bucket(d0), _t5_bucket(d0 + MOBA_BLOCK)]).astype(I32)
    et = (jnp.arange(S)[:, None] // MOBA_BLOCK == jnp.arange(LANES)[None, :]).astype(BF16)
    bias = _moba_bias(rel_bias, bidx)

    h = x.reshape(T, D)
    for li in range(n_layers):
        w = w_in[li]
        w_main = jnp.concatenate([w[:, :o_kr], w[:, o_q:]], axis=1).astype(BF16)
        w_kr = jnp.pad(w[:, o_kr:o_q], ((0, 0), (0, LANES - MLA_ROPE))).astype(BF16)
        wq = w_uq[li].reshape(MLA_Q_LORA, H, MLA_NOPE + MLA_ROPE)
        wq = jnp.pad(wq, ((0, 0), (0, 0), (0, MLA_SLOT - MLA_NOPE - MLA_ROPE)))
        wq = wq.reshape(MLA_Q_LORA, H * MLA_SLOT).astype(BF16)
        wkv = w_ukv[li].reshape(MLA_KV_LORA, H, MLA_NOPE + MLA_V)
        wk = wkv[:, :, :MLA_NOPE].reshape(MLA_KV_LORA, H * MLA_NOPE).astype(BF16)
        wv = wkv[:, :, MLA_NOPE:].reshape(MLA_KV_LORA, H * MLA_V).astype(BF16)
        wr = jnp.pad(w_router[li], ((0, 0), (0, LANES - E))).astype(BF16)
        br = jnp.pad(b_router[li], (0, LANES - E), constant_values=NEG_INF)[None, :]

        lat, qkv, gates, kr = _in_proj(h, attn_norm[li][None, :], w_main, w_kr, o_kr, 3 * mw)
        q_a, k_a, v_a = _mla_proj(lat, kr, cosw, sinw, q_lat_norm[li][None, :], kv_lat_norm[li][None, :],
                                  wq, wk, wv, S)
        y_a = _mla_attn(q_a, k_a, v_a, B, S)
        y_b = _moba_attn(qkv, et, bias, rel_bias, B, S)
        h1, xn, logits = _out_proj(y_a, y_b, gates, h, w_o_mla[li].astype(BF16), w_o_moba[li].astype(BF16),
                                   w_out[li].astype(BF16), moe_norm[li][None, :], wr, br)

        e_k, w_k, r_k, cnt = _route(logits, E)
        counts = cnt[:E, 0].astype(I32)
        tiles = (counts + EXPERT_ROWS - 1) // EXPERT_ROWS
        tile_end = jnp.cumsum(tiles)
        tile_start = tile_end - tiles
        n_tiles = (T * TOP_K) // EXPERT_ROWS + E
        n_used = tile_end[-1]
        tile_ids = jnp.arange(n_tiles)
        capped = jnp.minimum(tile_ids, n_used - 1)
        tile_expert = jnp.minimum(jnp.sum(tile_end[None, :] <= capped[:, None], axis=1), E - 1).astype(I32)
        base = jnp.pad(tile_start * EXPERT_ROWS, (0, LANES - E)).astype(I32)
        dest = _dest(e_k, r_k, jnp.broadcast_to(base[:, None], (LANES, LANES)))
        mine = tile_expert[:, None] == jnp.arange(E)[None, :]
        in_tile = jnp.sum(jnp.where(mine, counts[None, :] - (tile_ids[:, None] - tile_start[None, :]) * EXPERT_ROWS, 0),
                          axis=1)
        tile_valid = jnp.where(tile_ids < n_used, jnp.clip(in_tile, 0, EXPERT_ROWS), 0).astype(I32)
        chunk_lo = jnp.arange(n_tiles * EXPERT_ROWS // ZERO_ROWS) * ZERO_ROWS
        pad_lo = tile_start * EXPERT_ROWS + counts
        pad_hi = tile_end * EXPERT_ROWS
        in_pad = (chunk_lo[:, None] < pad_hi[None, :]) & (chunk_lo[:, None] + ZERO_ROWS > pad_lo[None, :])
        zflag = (jnp.any(in_pad, axis=1) | (chunk_lo >= n_used * EXPERT_ROWS)).astype(I32)
        nu = n_used.astype(I32)[None]

        xg = _dispatch(zflag, dest, xn)
        hid = _ffn_up(tile_expert, tile_valid, nu, xg, w_gate[li], w_up[li],
                      b_gate[li][:, None, :], b_up[li][:, None, :])
        yg = _ffn_down(tile_expert, tile_valid, nu, hid, w_down[li], b_down[li][:, None, :])
        h = _combine(dest, w_k[:TOP_K].T, h1, p[li].reshape(T, -1), w_ple[li].astype(BF16),
                     w_ple_gate[li].astype(BF16), ple_norm[li][None, :], final_norm[None, :], yg,
                     final=li == n_layers - 1)
    return h.reshape(B, S, D)
```
